```python
import jax, jax.numpy as jnp
from jax import lax
import numpy as np

D_MODEL = 1024
BATCH = 8
SEQ = 4096
DEPTH = 1

CHUNK = 64
N_META = 16

POOL_WIDTH = 512
POOL_WINDOWS = (2, 4, 8, 16)
POOL_GROUPS = len(POOL_WINDOWS)
POOL_GROUP_DIM = POOL_WIDTH // POOL_GROUPS
POOL_OUT_DIM = D_MODEL // POOL_GROUPS

CONV_WIDTH = 512
CONV_KERNEL = 31

N_BRANCHES = 2
IN_COLS = POOL_WIDTH + 2 * CONV_WIDTH + N_BRANCHES * D_MODEL

N_EXPERTS = 32
TOP_K = 4
D_EXPERT = D_MODEL
SWIGLU_LIMIT = 7.0
SWIGLU_ALPHA = 1.702
MOE_BLOCK = 256

RMS_EPS = 1e-5
LN_EPS = 1e-5

kernel_name = "hybrid_pool_conformerconv_moe_block"


def rmsnorm(x, g):
    xf = x.astype(jnp.float32)
    y = xf * lax.rsqrt(jnp.mean(xf * xf, axis=-1, keepdims=True) + RMS_EPS)
    return (y * g.astype(jnp.float32)).astype(x.dtype)


def pool_mixer(u, w_grp, scale):
    b, l, _ = u.shape
    uf = u.astype(jnp.float32)
    cs = jnp.cumsum(uf, axis=1)
    t = jnp.arange(1, l + 1, dtype=jnp.float32)[None, :, None]
    outs = []
    for g, w in enumerate(POOL_WINDOWS):
        c = cs[..., g * POOL_GROUP_DIM:(g + 1) * POOL_GROUP_DIM]
        c_prev = jnp.pad(c, ((0, 0), (w, 0), (0, 0)))[:, :l]
        mean = (c - c_prev) / jnp.minimum(t, float(w))
        outs.append(mean - uf[..., g * POOL_GROUP_DIM:(g + 1) * POOL_GROUP_DIM])
    d = jnp.stack(outs, axis=2).astype(u.dtype)
    y = jnp.einsum('blgc,gcd->blgd', d, w_grp).reshape(b, l, D_MODEL)
    return y * scale


def conformer_conv(v, gt, k_dw, b_dw, ln_g, ln_b, w_pw2, b_pw2):
    a = v * jax.nn.sigmoid(gt)
    c = lax.conv_general_dilated(
        a, k_dw[:, None, :], window_strides=(1,),
        padding=[(CONV_KERNEL - 1, 0)],
        dimension_numbers=('NWC', 'WIO', 'NWC'),
        feature_group_count=CONV_WIDTH) + b_dw
    cf = c.astype(jnp.float32)
    mu = jnp.mean(cf, axis=-1, keepdims=True)
    var = jnp.mean(jnp.square(cf - mu), axis=-1, keepdims=True)
    cn = ((cf - mu) * lax.rsqrt(var + LN_EPS) * ln_g.astype(jnp.float32)
          + ln_b.astype(jnp.float32)).astype(v.dtype)
    return jax.nn.silu(cn) @ w_pw2 + b_pw2


def moe_ffn(h, router_w, router_b, w_gu, b_gu, w_dn, b_dn):
    n_tok = h.shape[0]
    logits = h.astype(jnp.float32) @ router_w.astype(jnp.float32) + router_b.astype(jnp.float32)
    top_v, top_i = lax.top_k(logits, TOP_K)
    gates = jax.nn.softmax(top_v, axis=-1)

    n_assign = n_tok * TOP_K
    e_flat = top_i.reshape(-1)
    tok_flat = jnp.arange(n_assign, dtype=jnp.int32) // TOP_K
    g_flat = gates.reshape(-1)
    order = jnp.argsort(e_flat)
    se, stok, sg = e_flat[order], tok_flat[order], g_flat[order]

    counts = jnp.bincount(e_flat, length=N_EXPERTS)
    padded = (counts + MOE_BLOCK - 1) // MOE_BLOCK * MOE_BLOCK
    start = jnp.cumsum(counts) - counts
    pend = jnp.cumsum(padded)
    pstart = pend - padded
    dest = pstart[se] + jnp.arange(n_assign, dtype=jnp.int32) - start[se]

    n_blocks = -(-n_assign // MOE_BLOCK) + N_EXPERTS
    n_slots = n_blocks * MOE_BLOCK
    slot_tok = jnp.zeros((n_slots,), jnp.int32).at[dest].set(stok)
    slot_g = jnp.zeros((n_slots,), jnp.float32).at[dest].set(sg)
    block_e = jnp.minimum(
        jnp.searchsorted(pend, jnp.arange(n_blocks, dtype=pend.dtype) * MOE_BLOCK, side='right'),
        N_EXPERTS - 1)

    def expert_block(args):
        idx, gw, e = args
        xb = h[idx]
        gu = xb @ w_gu[e] + b_gu[e]
        gate, up = gu[:, :D_EXPERT], gu[:, D_EXPERT:]
        gate = jnp.minimum(gate, SWIGLU_LIMIT)
        up = jnp.clip(up, -SWIGLU_LIMIT, SWIGLU_LIMIT)
        act = (up + 1.0) * (gate * jax.nn.sigmoid(gate * SWIGLU_ALPHA))
        y = act @ w_dn[e] + b_dn[e]
        return y * gw[:, None].astype(y.dtype)

    y = lax.map(expert_block, (slot_tok.reshape(n_blocks, MOE_BLOCK),
                               slot_g.reshape(n_blocks, MOE_BLOCK), block_e))
    return jax.ops.segment_sum(y.reshape(n_slots, D_MODEL), slot_tok, num_segments=n_tok)


def setup_inputs(seed: int = 0) -> dict:
    key = jax.random.key(seed)
    ks = jax.random.split(key, 24)
    f32 = jnp.float32
    n = lambda k, s, sc: jax.random.normal(k, s, f32) * sc
    L_ = DEPTH
    return {
        "x": n(ks[0], (BATCH, SEQ, D_MODEL), 1.0),
        "meta_tokens": n(ks[1], (N_META, D_MODEL), 1.0),
        "norm_mix_g": 1.0 + n(ks[2], (L_, D_MODEL), 0.05),
        "w_in": n(ks[3], (L_, D_MODEL, IN_COLS), D_MODEL ** -0.5),
        "b_in": n(ks[4], (L_, IN_COLS), 0.02),
        "w_pool_grp": n(ks[5], (L_, POOL_GROUPS, POOL_GROUP_DIM, POOL_OUT_DIM), POOL_GROUP_DIM ** -0.5),
        "pool_scale": 1.0 + n(ks[6], (L_, D_MODEL), 0.1),
        "w_dwconv": n(ks[7], (L_, CONV_KERNEL, CONV_WIDTH), CONV_KERNEL ** -0.5),
        "b_dwconv": n(ks[8], (L_, CONV_WIDTH), 0.02),
        "conv_ln_g": 1.0 + n(ks[9], (L_, CONV_WIDTH), 0.05),
        "conv_ln_b": n(ks[10], (L_, CONV_WIDTH), 0.02),
        "w_pw2": n(ks[11], (L_, CONV_WIDTH, D_MODEL), CONV_WIDTH ** -0.5),
        "b_pw2": n(ks[12], (L_, D_MODEL), 0.02),
        "w_out": n(ks[13], (L_, D_MODEL, D_MODEL), D_MODEL ** -0.5),
        "norm_ffn_g": 1.0 + n(ks[14], (L_, D_MODEL), 0.05),
        "router_w": n(ks[15], (L_, D_MODEL, N_EXPERTS), D_MODEL ** -0.5),
        "router_b": n(ks[16], (L_, N_EXPERTS), 0.01),
        "w_gate_up": n(ks[17], (L_, N_EXPERTS, D_MODEL, 2 * D_EXPERT), D_MODEL ** -0.5),
        "b_gate_up": n(ks[18], (L_, N_EXPERTS, 2 * D_EXPERT), 0.02),
        "w_down": n(ks[19], (L_, N_EXPERTS, D_EXPERT, D_MODEL), D_EXPERT ** -0.5),
        "b_down": n(ks[20], (L_, N_EXPERTS, D_MODEL), 0.02),
        "norm_final_g": 1.0 + n(ks[21], (D_MODEL,), 0.05),
    }


def reference(x, meta_tokens, norm_mix_g, w_in, b_in, w_pool_grp, pool_scale,
              w_dwconv, b_dwconv, conv_ln_g, conv_ln_b, w_pw2, b_pw2, w_out,
              norm_ffn_g, router_w, router_b, w_gate_up, b_gate_up, w_down, b_down,
              norm_final_g):
    bsz, seq, d = x.shape
    meta = jnp.broadcast_to(meta_tokens[None].astype(x.dtype), (bsz, N_META, d))
    h = jnp.concatenate([meta, x], axis=1)
    length = seq + N_META
    c0 = POOL_WIDTH
    c1 = c0 + CONV_WIDTH
    c2 = c1 + CONV_WIDTH
    c3 = c2 + D_MODEL
    for l in range(DEPTH):
        xn = rmsnorm(h, norm_mix_g[l])
        proj = xn @ w_in[l] + b_in[l]
        u = proj[..., :c0]
        v = proj[..., c0:c1]
        gt = proj[..., c1:c2]
        g_a = proj[..., c2:c3]
        g_b = proj[..., c3:]
        y_a = pool_mixer(u, w_pool_grp[l], pool_scale[l])
        y_b = conformer_conv(v, gt, w_dwconv[l], b_dwconv[l], conv_ln_g[l], conv_ln_b[l],
                             w_pw2[l], b_pw2[l])
        mixed = jax.nn.sigmoid(g_a) * y_a + jax.nn.sigmoid(g_b) * y_b
        h = h + mixed @ w_out[l]
        hn = rmsnorm(h, norm_ffn_g[l]).reshape(bsz * length, d)
        h = h + moe_ffn(hn, router_w[l], router_b[l], w_gate_up[l], b_gate_up[l],
                        w_down[l], b_down[l]).reshape(bsz, length, d)
    h = rmsnorm(h, norm_final_g)
    return h[:, N_META:]
```

```python
import functools

import jax
import jax.numpy as jnp
from jax import lax
from jax.experimental import pallas as pl
from jax.experimental.pallas import tpu as pltpu

N_META = 16
POOL_WINDOWS = (2, 4, 8, 16)
POOL_GROUP_DIM = 128
POOL_OUT_DIM = 256
POOL_WIDTH = POOL_GROUP_DIM * len(POOL_WINDOWS)
CONV_WIDTH = 512
CONV_KERNEL = 31
N_EXPERTS = 32
TOP_K = 4
SWIGLU_LIMIT = 7.0
SWIGLU_ALPHA = 1.702
RMS_EPS = 1e-5
LN_EPS = 1e-5

LANES = 128
SUBLANES = 8

MIX_TILE = 512
POOL_HALO = 16
CONV_HALO = 32
CONV_ROWS = 64
ROW_TILE = 256
EXPERT_BLOCK = 256
VMEM_LIMIT = 56 * 1024 * 1024

_F32 = jnp.float32
_BF16 = jnp.bfloat16


def _rms(x, g):
    return x * lax.rsqrt(jnp.mean(x * x, axis=-1, keepdims=True) + RMS_EPS) * g


def _sigmoid(x):
    return 1.0 / (1.0 + jnp.exp(-x))


def _dot(a, b):
    return jnp.dot(a, b, preferred_element_type=_F32)


def _mixer_kernel(x_ref, meta_ref, gmix_ref, win_ref, bin_ref, wpool_ref, pscale_ref,
                  kdw_ref, bdw_ref, lng_ref, lnb_ref, wpw2_ref, bpw2_ref, wout_ref,
                  gffn_ref, wr_ref, rb_ref, tri_ref,
                  h1_ref, hn_ref, eid_ref, rank_ref, gate_ref, cnt_ref,
                  ubuf, abuf, base):
    b = pl.program_id(0)
    j = pl.program_id(1)
    tq = x_ref.shape[1]
    c0 = POOL_WIDTH
    c1 = c0 + CONV_WIDTH
    c2 = c1 + CONV_WIDTH
    d_model = x_ref.shape[2]
    c3 = c2 + d_model
    n_slab = CONV_WIDTH // LANES

    @pl.when((b == 0) & (j == 0))
    def _():
        base[...] = jnp.zeros_like(base)

    @pl.when(j == 0)
    def _():
        xm = _rms(meta_ref[...], gmix_ref[...]).astype(_BF16)
        pm = _dot(xm, win_ref[:, 0:c2]) + bin_ref[:, 0:c2]
        um = pm[:, 0:c0]
        am = pm[:, c0:c1] * _sigmoid(pm[:, c1:c2])
        for s in range(n_slab):
            ubuf[s, 0:POOL_HALO, :] = um[:, s * LANES:(s + 1) * LANES]
            abuf[s, 0:CONV_HALO - N_META, :] = jnp.zeros((CONV_HALO - N_META, LANES), _F32)
            abuf[s, CONV_HALO - N_META:CONV_HALO, :] = am[:, s * LANES:(s + 1) * LANES]

    x = x_ref[0]
    xn = _rms(x, gmix_ref[...]).astype(_BF16)

    u = _dot(xn, win_ref[:, 0:c0]) + bin_ref[:, 0:c0]
    v = _dot(xn, win_ref[:, c0:c1]) + bin_ref[:, c0:c1]
    gt = _dot(xn, win_ref[:, c1:c2]) + bin_ref[:, c1:c2]
    a = v * _sigmoid(gt)
    for s in range(n_slab):
        ubuf[s, POOL_HALO:POOL_HALO + tq, :] = u[:, s * LANES:(s + 1) * LANES]
        abuf[s, CONV_HALO:CONV_HALO + tq, :] = a[:, s * LANES:(s + 1) * LANES]

    ya = []
    for g, w in enumerate(POOL_WINDOWS):
        ug = ubuf[g, POOL_HALO:POOL_HALO + tq, :]
        acc = ug
        for back in range(1, w):
            acc = acc + ubuf[g, POOL_HALO - back:POOL_HALO - back + tq, :]
        dg = acc * (1.0 / w) - ug
        ya.append(_dot(dg.astype(_BF16), wpool_ref[g]))
    y_a = jnp.concatenate(ya, axis=-1) * pscale_ref[...]

    conv = []
    for s in range(n_slab):
        lo = s * LANES
        rows = []
        for r0 in range(0, tq, CONV_ROWS):
            acc = jnp.broadcast_to(bdw_ref[:, lo:lo + LANES], (CONV_ROWS, LANES))
            for tap in range(CONV_KERNEL):
                start = CONV_HALO - (CONV_KERNEL - 1) + tap + r0
                acc = acc + kdw_ref[tap:tap + 1, lo:lo + LANES] * abuf[s, start:start + CONV_ROWS, :]
            rows.append(acc)
        conv.append(jnp.concatenate(rows, axis=0))
    c = jnp.concatenate(conv, axis=-1)

    mu = jnp.mean(c, axis=-1, keepdims=True)
    cc = c - mu
    var = jnp.mean(cc * cc, axis=-1, keepdims=True)
    cn = cc * lax.rsqrt(var + LN_EPS) * lng_ref[...] + lnb_ref[...]
    sw = cn * _sigmoid(cn)
    y_b = _dot(sw.astype(_BF16), wpw2_ref[...]) + bpw2_ref[...]

    g_a = _dot(xn, win_ref[:, c2:c3]) + bin_ref[:, c2:c3]
    g_b = _dot(xn, win_ref[:, c3:]) + bin_ref[:, c3:]
    mixed = _sigmoid(g_a) * y_a + _sigmoid(g_b) * y_b
    h1 = x + _dot(mixed.astype(_BF16), wout_ref[...])
    h1_ref[...] = h1
    hn = _rms(h1, gffn_ref[...])
    hn_ref[...] = hn

    for s in range(n_slab):
        ubuf[s, 0:POOL_HALO, :] = ubuf[s, tq:tq + POOL_HALO, :]
        abuf[s, 0:CONV_HALO, :] = abuf[s, tq:tq + CONV_HALO, :]

    hn_hi = hn.astype(_BF16)
    hn_lo = (hn - hn_hi.astype(_F32)).astype(_BF16)
    nt_dims = (((1,), (1,)), ((), ()))
    l_hi = lax.dot_general(wr_ref[...], hn_hi, nt_dims, preferred_element_type=_F32)
    l_lo = lax.dot_general(wr_ref[0:N_EXPERTS, :], hn_lo, nt_dims, preferred_element_type=_F32)
    logits = l_hi[0:N_EXPERTS] + l_hi[N_EXPERTS:2 * N_EXPERTS] + l_lo + rb_ref[...]

    eidx = lax.broadcasted_iota(jnp.int32, (N_EXPERTS, tq), 0).astype(_F32)
    work = logits
    vals, ids, hots = [], [], []
    for _ in range(TOP_K):
        m = jnp.max(work, axis=0, keepdims=True)
        first = jnp.min(jnp.where(work == m, eidx, float(N_EXPERTS)), axis=0, keepdims=True)
        hot = eidx == first
        vals.append(m)
        ids.append(first)
        hots.append(hot)
        work = jnp.where(hot, -jnp.inf, work)
    exps = [jnp.exp(vk - vals[0]) for vk in vals]
    denom = exps[0] + exps[1] + exps[2] + exps[3]
    gate_ref[0] = jnp.concatenate([ek / denom for ek in exps], axis=0)
    eid_ref[0] = jnp.concatenate(ids, axis=0).astype(jnp.int32)

    chosen = jnp.where(hots[0] | hots[1] | hots[2] | hots[3], 1.0, 0.0)
    before = _dot(chosen.astype(_BF16), tri_ref[...]) + base[:, 0:1]
    ranks = [jnp.sum(jnp.where(hk, before, 0.0), axis=0, keepdims=True) for hk in hots]
    rank_ref[0] = jnp.concatenate(ranks, axis=0).astype(jnp.int32)
    base[...] = base[...] + jnp.sum(chosen, axis=1, keepdims=True)
    cnt_ref[...] = base[...]


def _mixer_call(x, meta, gmix, win, bin_, wpool, pscale, kdw, bdw, lng, lnb, wpw2, bpw2,
                wout, gffn, wr, rb, tri, tq):
    bsz, seq, d = x.shape
    nj = seq // tq
    n_tok = bsz * seq
    n_tiles = bsz * nj

    def full(arr):
        nd = arr.ndim
        return pl.BlockSpec(arr.shape, lambda b, j, _n=nd: (0,) * _n)

    tile3 = pl.BlockSpec((1, TOP_K, tq), lambda b, j: (b * nj + j, 0, 0))
    rows = pl.BlockSpec((tq, d), lambda b, j: (b * nj + j, 0))
    consts = (meta, gmix, win, bin_, wpool, pscale, kdw, bdw, lng, lnb, wpw2, bpw2, wout,
              gffn, wr, rb, tri)
    return pl.pallas_call(
        _mixer_kernel,
        grid=(bsz, nj),
        in_specs=[pl.BlockSpec((1, tq, d), lambda b, j: (b, j, 0))] + [full(c) for c in consts],
        out_specs=[rows, rows, tile3, tile3, tile3,
                   pl.BlockSpec((N_EXPERTS, LANES), lambda b, j: (0, 0))],
        out_shape=[jax.ShapeDtypeStruct((n_tok, d), _F32),
                   jax.ShapeDtypeStruct((n_tok, d), _F32),
                   jax.ShapeDtypeStruct((n_tiles, TOP_K, tq), jnp.int32),
                   jax.ShapeDtypeStruct((n_tiles, TOP_K, tq), jnp.int32),
                   jax.ShapeDtypeStruct((n_tiles, TOP_K, tq), _F32),
                   jax.ShapeDtypeStruct((N_EXPERTS, LANES), _F32)],
        scratch_shapes=[pltpu.VMEM((CONV_WIDTH // LANES, POOL_HALO + tq, LANES), _F32),
                        pltpu.VMEM((CONV_WIDTH // LANES, CONV_HALO + tq, LANES), _F32),
                        pltpu.VMEM((N_EXPERTS, LANES), _F32)],
        compiler_params=pltpu.CompilerParams(
            dimension_semantics=("arbitrary", "arbitrary"), vmem_limit_bytes=VMEM_LIMIT),
        name="mixer",
    )(x, *consts)


def _dispatch_kernel(fill_ref, pos_ref, hn_ref, xs_ref, zblk, sem, zsem, *, bm):
    i = pl.program_id(0)
    tr = hn_ref.shape[0]
    n_blocks = xs_ref.shape[0] // bm

    def issue(t, carry):
        for k in range(TOP_K):
            p = pos_ref[0, k, t]
            pltpu.make_async_copy(hn_ref.at[pl.ds(t, 1)], xs_ref.at[pl.ds(p, 1)], sem).start()
        return carry

    lax.fori_loop(0, tr, issue, 0)

    @pl.when(i == pl.num_programs(0) - 1)
    def _():
        zblk[...] = jnp.zeros_like(zblk)

        def zero_row(r, carry):
            pltpu.make_async_copy(zblk.at[pl.ds(0, 1)], xs_ref.at[pl.ds(r, 1)], zsem).start()
            return carry

        def drain_row(r, carry):
            pltpu.make_async_copy(zblk.at[pl.ds(0, 1)], xs_ref.at[pl.ds(r, 1)], zsem).wait()
            return carry

        for e in range(N_EXPERTS):
            lax.fori_loop(fill_ref[0, e], fill_ref[1, e], zero_row, 0)
            lax.fori_loop(fill_ref[0, e], fill_ref[1, e], drain_row, 0)

        def zero_blk(q, carry):
            start = pl.multiple_of(q * bm, bm)
            pltpu.make_async_copy(zblk, xs_ref.at[pl.ds(start, bm)], zsem).start()
            return carry

        def drain_blk(q, carry):
            start = pl.multiple_of(q * bm, bm)
            pltpu.make_async_copy(zblk, xs_ref.at[pl.ds(start, bm)], zsem).wait()
            return carry

        lax.fori_loop(fill_ref[2, 0], n_blocks, zero_blk, 0)
        lax.fori_loop(fill_ref[2, 0], n_blocks, drain_blk, 0)

    for k in range(TOP_K):
        pltpu.make_async_copy(hn_ref, xs_ref.at[pl.ds(0, tr)], sem).wait()


def _dispatch_call(fill, pos, hn, n_slots, tr, bm):
    n_tok, d = hn.shape
    return pl.pallas_call(
        functools.partial(_dispatch_kernel, bm=bm),
        grid_spec=pltpu.PrefetchScalarGridSpec(
            num_scalar_prefetch=1,
            grid=(n_tok // tr,),
            in_specs=[pl.BlockSpec((1, TOP_K, tr), lambda i, f: (i, 0, 0), memory_space=pltpu.SMEM),
                      pl.BlockSpec((tr, d), lambda i, f: (i, 0))],
            out_specs=pl.BlockSpec(memory_space=pl.ANY),
            scratch_shapes=[pltpu.VMEM((bm, d), _F32), pltpu.SemaphoreType.DMA,
                            pltpu.SemaphoreType.DMA]),
        out_shape=jax.ShapeDtypeStruct((n_slots, d), _F32),
        compiler_params=pltpu.CompilerParams(dimension_semantics=("arbitrary",)),
        name="dispatch",
    )(fill, pos, hn)


def _experts_kernel(be_ref, nv_ref, xs_ref, wgu_ref, bgu_ref, wdn_ref, bdn_ref,
                    ys_ref, wgu_bf, wdn_bf):
    i = pl.program_id(0)
    d_exp = wdn_ref.shape[1]
    prev = jnp.maximum(i - 1, 0)
    changed = (i == 0) | (be_ref[i] != be_ref[prev])

    @pl.when(changed)
    def _():
        wgu_bf[...] = wgu_ref[0].astype(_BF16)
        wdn_bf[...] = wdn_ref[0].astype(_BF16)

    @pl.when(i >= nv_ref[0])
    def _():
        ys_ref[...] = jnp.zeros_like(ys_ref)

    @pl.when(i < nv_ref[0])
    def _():
        xb = xs_ref[...].astype(_BF16)
        gu = _dot(xb, wgu_bf[...]) + bgu_ref[0]
        gate = jnp.minimum(gu[:, 0:d_exp], SWIGLU_LIMIT)
        up = jnp.clip(gu[:, d_exp:], -SWIGLU_LIMIT, SWIGLU_LIMIT)
        act = (up + 1.0) * (gate * _sigmoid(gate * SWIGLU_ALPHA))
        ys_ref[...] = _dot(act.astype(_BF16), wdn_bf[...]) + bdn_ref[0]


def _experts_call(blk_e, n_valid, xs, wgu, bgu, wdn, bdn, bm):
    n_slots, d = xs.shape
    n_blocks = n_slots // bm
    d_gu = wgu.shape[2]
    d_exp = wdn.shape[1]
    return pl.pallas_call(
        _experts_kernel,
        grid_spec=pltpu.PrefetchScalarGridSpec(
            num_scalar_prefetch=2,
            grid=(n_blocks,),
            in_specs=[pl.BlockSpec((bm, d), lambda i, be, nv: (i, 0)),
                      pl.BlockSpec((1, d, d_gu), lambda i, be, nv: (be[i], 0, 0)),
                      pl.BlockSpec((1, 1, d_gu), lambda i, be, nv: (be[i], 0, 0)),
                      pl.BlockSpec((1, d_exp, d), lambda i, be, nv: (be[i], 0, 0)),
                      pl.BlockSpec((1, 1, d), lambda i, be, nv: (be[i], 0, 0))],
            out_specs=pl.BlockSpec((bm, d), lambda i, be, nv: (i, 0)),
            scratch_shapes=[pltpu.VMEM((d, d_gu), _BF16), pltpu.VMEM((d_exp, d), _BF16)]),
        out_shape=jax.ShapeDtypeStruct((n_slots, d), _F32),
        compiler_params=pltpu.CompilerParams(
            dimension_semantics=("arbitrary",), vmem_limit_bytes=VMEM_LIMIT),
        name="experts",
    )(blk_e, n_valid, xs, wgu, bgu, wdn, bdn)


def _combine_kernel(pos_ref, gate_ref, h1_ref, gfin_ref, ys_ref, out_ref, buf, sem):
    tr = h1_ref.shape[0]

    def issue(t, carry):
        for k in range(TOP_K):
            p = pos_ref[0, k, t]
            pltpu.make_async_copy(ys_ref.at[pl.ds(p, 1)], buf.at[k, pl.ds(t, 1)], sem).start()
        return carry

    lax.fori_loop(0, tr, issue, 0)
    for k in range(TOP_K):
        pltpu.make_async_copy(ys_ref.at[pl.ds(0, tr)], buf.at[k], sem).wait()

    acc = h1_ref[...]
    for k in range(TOP_K):
        acc = acc + buf[k] * gate_ref[:, k:k + 1]
    out_ref[...] = _rms(acc, gfin_ref[...])


def _combine_call(pos, gate_t, h1, gfin, ys, tr):
    n_tok, d = h1.shape
    return pl.pallas_call(
        _combine_kernel,
        grid=(n_tok // tr,),
        in_specs=[pl.BlockSpec((1, TOP_K, tr), lambda i: (i, 0, 0), memory_space=pltpu.SMEM),
                  pl.BlockSpec((tr, TOP_K), lambda i: (i, 0)),
                  pl.BlockSpec((tr, d), lambda i: (i, 0)),
                  pl.BlockSpec((1, d), lambda i: (0, 0)),
                  pl.BlockSpec(memory_space=pl.ANY)],
        out_specs=pl.BlockSpec((tr, d), lambda i: (i, 0)),
        out_shape=jax.ShapeDtypeStruct((n_tok, d), _F32),
        scratch_shapes=[pltpu.VMEM((TOP_K, tr, d), _F32), pltpu.SemaphoreType.DMA],
        compiler_params=pltpu.CompilerParams(dimension_semantics=("arbitrary",)),
        name="combine",
    )(pos, gate_t, h1, gfin, ys)


def _retile(a, tr):
    n_tiles, k, tq = a.shape
    a = a.transpose(1, 0, 2).reshape(k, n_tiles * tq // tr, tr)
    return a.transpose(1, 0, 2)


def kernel(x, meta_tokens, norm_mix_g, w_in, b_in, w_pool_grp, pool_scale, w_dwconv, b_dwconv, conv_ln_g, conv_ln_b, w_pw2, b_pw2, w_out, norm_ffn_g, router_w, router_b, w_gate_up, b_gate_up, w_down, b_down, norm_final_g):
    bsz, seq, d = x.shape
    assert w_in.shape[0] == 1, "one layer"
    tq = min(MIX_TILE, seq)
    tr = min(ROW_TILE, seq)
    bm = EXPERT_BLOCK
    n_tok = bsz * seq
    n_assign = n_tok * TOP_K
    n_blocks = n_assign // bm + N_EXPERTS
    n_slots = n_blocks * bm

    row = lambda a: a.reshape(1, -1)
    wr_t = router_w[0].T
    wr_hi = wr_t.astype(_BF16)
    wr_lo = (wr_t - wr_hi.astype(_F32)).astype(_BF16)
    tri = (jnp.arange(tq)[:, None] < jnp.arange(tq)[None, :]).astype(_BF16)

    h1, hn, eid, rank, gate, cnt = _mixer_call(
        x, meta_tokens, row(norm_mix_g[0]), w_in[0].astype(_BF16), row(b_in[0]),
        w_pool_grp[0].astype(_BF16), row(pool_scale[0]), w_dwconv[0], row(b_dwconv[0]),
        row(conv_ln_g[0]), row(conv_ln_b[0]), w_pw2[0].astype(_BF16), row(b_pw2[0]),
        w_out[0].astype(_BF16), row(norm_ffn_g[0]), jnp.concatenate([wr_hi, wr_lo], axis=0),
        router_b[0].reshape(N_EXPERTS, 1), tri, tq)

    counts = cnt[:, 0].astype(jnp.int32)
    padded = (counts + bm - 1) // bm * bm
    pend = jnp.cumsum(padded)
    pstart = pend - padded
    n_valid = pend[-1] // bm
    blk = jnp.minimum(jnp.arange(n_blocks, dtype=jnp.int32), n_valid - 1)
    blk_e = jnp.sum((pend[None, :] <= (blk * bm)[:, None]).astype(jnp.int32), axis=1)
    blk_e = jnp.minimum(blk_e, N_EXPERTS - 1)
    pos = pstart[eid] + rank
    fill = jnp.stack([pstart + counts, pend, jnp.full_like(pend, n_valid)]).astype(jnp.int32)

    pos_r = _retile(pos, tr)
    gate_t = gate.transpose(0, 2, 1).reshape(n_tok, TOP_K)

    xs = _dispatch_call(fill, pos_r, hn, n_slots, tr, bm)
    ys = _experts_call(blk_e, n_valid.reshape(1).astype(jnp.int32), xs,
                       w_gate_up[0], b_gate_up[0].reshape(N_EXPERTS, 1, -1),
                       w_down[0], b_down[0].reshape(N_EXPERTS, 1, -1), bm)
    out = _combine_call(pos_r, gate_t, h1, row(norm_final_g), ys, tr)
    return out.reshape(bsz, seq, d)
```

```python
import functools

import jax
import jax.numpy as jnp
from jax import lax
from jax.experimental import pallas as pl
from jax.experimental.pallas import tpu as pltpu

N_META = 16
POOL_WINDOWS = (2, 4, 8, 16)
POOL_GROUP_DIM = 128
POOL_OUT_DIM = 256
POOL_WIDTH = POOL_GROUP_DIM * len(POOL_WINDOWS)
CONV_WIDTH = 512
CONV_KERNEL = 31
N_EXPERTS = 32
TOP_K = 4
SWIGLU_LIMIT = 7.0
SWIGLU_ALPHA = 1.702
RMS_EPS = 1e-5
LN_EPS = 1e-5

LANES = 128
SUBLANES = 8

MIX_TILE = 512
POOL_HALO = 16
CONV_HALO = 32
CONV_ROWS = 64
SORT_TILE = 256
EXPERT_BLOCK = 256
VMEM_LIMIT = 56 * 1024 * 1024

_F32 = jnp.float32
_BF16 = jnp.bfloat16


def _rms(x, g):
    return x * lax.rsqrt(jnp.mean(x * x, axis=-1, keepdims=True) + RMS_EPS) * g


def _sigmoid(x):
    return 1.0 / (1.0 + jnp.exp(-x))


def _dot(a, b):
    return jnp.dot(a, b, preferred_element_type=_F32)


def _mixer_kernel(x_ref, meta_ref, gmix_ref, win_ref, bin_ref, wpool_ref, pscale_ref,
                  kdw_ref, bdw_ref, lng_ref, lnb_ref, wpw2_ref, bpw2_ref, wout_ref,
                  gffn_ref, wr_ref, rb_ref, tri_ref, lmat_ref,
                  h1_ref, hn_ref, gate_ref, slot_ref, segend_ref, off_ref, tot_ref,
                  ubuf, abuf, base):
    b = pl.program_id(0)
    j = pl.program_id(1)
    tq = x_ref.shape[1]
    c0 = POOL_WIDTH
    c1 = c0 + CONV_WIDTH
    c2 = c1 + CONV_WIDTH
    d_model = x_ref.shape[2]
    c3 = c2 + d_model
    n_slab = CONV_WIDTH // LANES

    @pl.when((b == 0) & (j == 0))
    def _():
        base[...] = jnp.zeros_like(base)

    @pl.when(j == 0)
    def _():
        xm = _rms(meta_ref[...], gmix_ref[...]).astype(_BF16)
        pm = _dot(xm, win_ref[:, 0:c2]) + bin_ref[:, 0:c2]
        um = pm[:, 0:c0]
        am = pm[:, c0:c1] * _sigmoid(pm[:, c1:c2])
        for s in range(n_slab):
            ubuf[s, 0:POOL_HALO, :] = um[:, s * LANES:(s + 1) * LANES]
            abuf[s, 0:CONV_HALO - N_META, :] = jnp.zeros((CONV_HALO - N_META, LANES), _F32)
            abuf[s, CONV_HALO - N_META:CONV_HALO, :] = am[:, s * LANES:(s + 1) * LANES]

    x = x_ref[0]
    xn = _rms(x, gmix_ref[...]).astype(_BF16)

    u = _dot(xn, win_ref[:, 0:c0]) + bin_ref[:, 0:c0]
    v = _dot(xn, win_ref[:, c0:c1]) + bin_ref[:, c0:c1]
    gt = _dot(xn, win_ref[:, c1:c2]) + bin_ref[:, c1:c2]
    a = v * _sigmoid(gt)
    for s in range(n_slab):
        ubuf[s, POOL_HALO:POOL_HALO + tq, :] = u[:, s * LANES:(s + 1) * LANES]
        abuf[s, CONV_HALO:CONV_HALO + tq, :] = a[:, s * LANES:(s + 1) * LANES]

    ya = []
    for g, w in enumerate(POOL_WINDOWS):
        ug = ubuf[g, POOL_HALO:POOL_HALO + tq, :]
        acc = ug
        for back in range(1, w):
            acc = acc + ubuf[g, POOL_HALO - back:POOL_HALO - back + tq, :]
        dg = acc * (1.0 / w) - ug
        ya.append(_dot(dg.astype(_BF16), wpool_ref[g]))
    y_a = jnp.concatenate(ya, axis=-1) * pscale_ref[...]

    conv = []
    for s in range(n_slab):
        lo = s * LANES
        rows = []
        for r0 in range(0, tq, CONV_ROWS):
            acc = jnp.broadcast_to(bdw_ref[:, lo:lo + LANES], (CONV_ROWS, LANES))
            for tap in range(CONV_KERNEL):
                start = CONV_HALO - (CONV_KERNEL - 1) + tap + r0
                acc = acc + kdw_ref[tap:tap + 1, lo:lo + LANES] * abuf[s, start:start + CONV_ROWS, :]
            rows.append(acc)
        conv.append(jnp.concatenate(rows, axis=0))
    c = jnp.concatenate(conv, axis=-1)

    mu = jnp.mean(c, axis=-1, keepdims=True)
    cc = c - mu
    var = jnp.mean(cc * cc, axis=-1, keepdims=True)
    cn = cc * lax.rsqrt(var + LN_EPS) * lng_ref[...] + lnb_ref[...]
    sw = cn * _sigmoid(cn)
    y_b = _dot(sw.astype(_BF16), wpw2_ref[...]) + bpw2_ref[...]

    g_a = _dot(xn, win_ref[:, c2:c3]) + bin_ref[:, c2:c3]
    g_b = _dot(xn, win_ref[:, c3:]) + bin_ref[:, c3:]
    mixed = _sigmoid(g_a) * y_a + _sigmoid(g_b) * y_b
    h1 = x + _dot(mixed.astype(_BF16), wout_ref[...])
    h1_ref[...] = h1
    hn = _rms(h1, gffn_ref[...])
    hn_ref[...] = hn

    for s in range(n_slab):
        ubuf[s, 0:POOL_HALO, :] = ubuf[s, tq:tq + POOL_HALO, :]
        abuf[s, 0:CONV_HALO, :] = abuf[s, tq:tq + CONV_HALO, :]

    hn_hi = hn.astype(_BF16)
    hn_lo = (hn - hn_hi.astype(_F32)).astype(_BF16)
    nt_dims = (((1,), (1,)), ((), ()))
    l_hi = lax.dot_general(wr_ref[...], hn_hi, nt_dims, preferred_element_type=_F32)
    l_lo = lax.dot_general(wr_ref[0:N_EXPERTS, :], hn_lo, nt_dims, preferred_element_type=_F32)
    logits = l_hi[0:N_EXPERTS] + l_hi[N_EXPERTS:2 * N_EXPERTS] + l_lo + rb_ref[...]

    eidx = lax.broadcasted_iota(jnp.int32, (N_EXPERTS, tq), 0).astype(_F32)
    work = logits
    vals, hots = [], []
    for _ in range(TOP_K):
        m = jnp.max(work, axis=0, keepdims=True)
        first = jnp.min(jnp.where(work == m, eidx, float(N_EXPERTS)), axis=0, keepdims=True)
        hot = eidx == first
        vals.append(m)
        hots.append(hot)
        work = jnp.where(hot, -jnp.inf, work)
    exps = [jnp.exp(vk - vals[0]) for vk in vals]
    denom = exps[0] + exps[1] + exps[2] + exps[3]
    gate_ref[0] = jnp.concatenate([ek / denom for ek in exps], axis=0)

    ts = tri_ref.shape[0]
    for h in range(tq // ts):
        hk = [hot[:, h * ts:(h + 1) * ts] for hot in hots]
        chosen = jnp.where(hk[0] | hk[1] | hk[2] | hk[3], 1.0, 0.0)
        groups = jnp.floor((jnp.sum(chosen, axis=1, keepdims=True) + (SUBLANES - 1)) * (1.0 / SUBLANES))
        groups = jnp.broadcast_to(groups, (N_EXPERTS, LANES))
        seg = _dot(lmat_ref[...], groups.astype(_BF16)) * float(SUBLANES)
        place = seg[:, 0:1] + _dot(chosen.astype(_BF16), tri_ref[...])
        slots = [jnp.sum(jnp.where(m, place, 0.0), axis=0, keepdims=True) for m in hk]
        slot_ref[h] = jnp.concatenate(slots, axis=0).astype(jnp.int32)
        segend_ref[h] = seg + groups * float(SUBLANES)
        off_ref[h] = base[...] - seg
        base[...] = base[...] + groups * float(SUBLANES)
    tot_ref[...] = base[...]


def _mixer_call(x, meta, gmix, win, bin_, wpool, pscale, kdw, bdw, lng, lnb, wpw2, bpw2,
                wout, gffn, wr, rb, tri, lmat, tq):
    bsz, seq, d = x.shape
    nj = seq // tq
    n_tok = bsz * seq
    n_tiles = bsz * nj
    ts = tri.shape[0]
    n_sub = tq // ts

    def full(arr):
        nd = arr.ndim
        return pl.BlockSpec(arr.shape, lambda b, j, _n=nd: (0,) * _n)

    def tile3(k, w):
        return pl.BlockSpec((k, TOP_K, w), lambda b, j: (b * nj + j, 0, 0))

    per_expert = pl.BlockSpec((n_sub, N_EXPERTS, LANES), lambda b, j: (b * nj + j, 0, 0))
    rows = pl.BlockSpec((tq, d), lambda b, j: (b * nj + j, 0))
    consts = (meta, gmix, win, bin_, wpool, pscale, kdw, bdw, lng, lnb, wpw2, bpw2, wout,
              gffn, wr, rb, tri, lmat)
    return pl.pallas_call(
        _mixer_kernel,
        grid=(bsz, nj),
        in_specs=[pl.BlockSpec((1, tq, d), lambda b, j: (b, j, 0))] + [full(c) for c in consts],
        out_specs=[rows, rows, tile3(1, tq), tile3(n_sub, ts), per_expert, per_expert,
                   pl.BlockSpec((N_EXPERTS, LANES), lambda b, j: (0, 0))],
        out_shape=[jax.ShapeDtypeStruct((n_tok, d), _F32),
                   jax.ShapeDtypeStruct((n_tok, d), _F32),
                   jax.ShapeDtypeStruct((n_tiles, TOP_K, tq), _F32),
                   jax.ShapeDtypeStruct((n_tiles * n_sub, TOP_K, ts), jnp.int32),
                   jax.ShapeDtypeStruct((n_tiles * n_sub, N_EXPERTS, LANES), _F32),
                   jax.ShapeDtypeStruct((n_tiles * n_sub, N_EXPERTS, LANES), _F32),
                   jax.ShapeDtypeStruct((N_EXPERTS, LANES), _F32)],
        scratch_shapes=[pltpu.VMEM((CONV_WIDTH // LANES, POOL_HALO + tq, LANES), _F32),
                        pltpu.VMEM((CONV_WIDTH // LANES, CONV_HALO + tq, LANES), _F32),
                        pltpu.VMEM((N_EXPERTS, LANES), _F32)],
        compiler_params=pltpu.CompilerParams(
            dimension_semantics=("arbitrary", "arbitrary"), vmem_limit_bytes=VMEM_LIMIT),
        name="mixer",
    )(x, *consts)


def _dispatch_kernel(fill_ref, dst_ref, slot_ref, hn_ref, xs_ref, sbuf, zblk, sem, zsem, *, bm):
    i = pl.program_id(0)
    n_sub, _, ts = slot_ref.shape
    ns = sbuf.shape[1]
    n_blocks = xs_ref.shape[0] // bm

    def zero_fill(lo, hi, rows):
        def start(q, carry):
            at = pl.multiple_of(q * rows, rows)
            pltpu.make_async_copy(zblk.at[pl.ds(0, rows)], xs_ref.at[pl.ds(at, rows)], zsem).start()
            return carry

        def drain(q, carry):
            at = pl.multiple_of(q * rows, rows)
            pltpu.make_async_copy(zblk.at[pl.ds(0, rows)], xs_ref.at[pl.ds(at, rows)], zsem).wait()
            return carry

        lax.fori_loop(lo, hi, start, 0)
        lax.fori_loop(lo, hi, drain, 0)

    @pl.when(i == 0)
    def _():
        zblk[...] = jnp.zeros_like(zblk)
        for e in range(N_EXPERTS):
            zero_fill(fill_ref[0, e], fill_ref[1, e], SUBLANES)
        zero_fill(fill_ref[2, 0], n_blocks, bm)

    for h in range(n_sub):
        @pl.when(i > 0)
        def _():
            pltpu.make_async_copy(sbuf.at[h], xs_ref.at[pl.ds(0, ns)], sem.at[h]).wait()

        slot = slot_ref[h]
        rows = lax.broadcasted_iota(jnp.int32, (ns, ts), 0)
        sel = jnp.where(rows == slot[0:1, :], 1.0, 0.0)
        for k in range(1, TOP_K):
            sel = sel + jnp.where(rows == slot[k:k + 1, :], 1.0, 0.0)
        sbuf[h] = _dot(sel.astype(_BF16), hn_ref[h * ts:(h + 1) * ts, :].astype(_BF16))

        def issue(q, carry):
            at = pl.multiple_of(q * SUBLANES, SUBLANES)
            to = pl.multiple_of(dst_ref[0, h, q], SUBLANES)
            pltpu.make_async_copy(sbuf.at[h, pl.ds(at, SUBLANES)],
                                  xs_ref.at[pl.ds(to, SUBLANES)], sem.at[h]).start()
            return carry

        lax.fori_loop(0, ns // SUBLANES, issue, 0)

    @pl.when(i == pl.num_programs(0) - 1)
    def _():
        for h in range(n_sub):
            pltpu.make_async_copy(sbuf.at[h], xs_ref.at[pl.ds(0, ns)], sem.at[h]).wait()


def _dispatch_call(fill, dst, slot, hn, n_rows, tq, ns, bm):
    n_tok, d = hn.shape
    n_sub, ts = dst.shape[1], slot.shape[2]
    return pl.pallas_call(
        functools.partial(_dispatch_kernel, bm=bm),
        grid_spec=pltpu.PrefetchScalarGridSpec(
            num_scalar_prefetch=1,
            grid=(n_tok // tq,),
            in_specs=[pl.BlockSpec((1, n_sub, ns // SUBLANES), lambda i, f: (i, 0, 0),
                                   memory_space=pltpu.SMEM),
                      pl.BlockSpec((n_sub, TOP_K, ts), lambda i, f: (i, 0, 0)),
                      pl.BlockSpec((tq, d), lambda i, f: (i, 0))],
            out_specs=pl.BlockSpec(memory_space=pl.ANY),
            scratch_shapes=[pltpu.VMEM((n_sub, ns, d), _F32), pltpu.VMEM((bm, d), _F32),
                            pltpu.SemaphoreType.DMA((n_sub,)), pltpu.SemaphoreType.DMA]),
        out_shape=jax.ShapeDtypeStruct((n_rows, d), _F32),
        compiler_params=pltpu.CompilerParams(
            dimension_semantics=("arbitrary",), vmem_limit_bytes=VMEM_LIMIT),
        name="dispatch",
    )(fill, dst, slot, hn)


def _experts_kernel(be_ref, nv_ref, xs_ref, wgu_ref, bgu_ref, wdn_ref, bdn_ref,
                    ys_ref, wgu_bf, wdn_bf):
    i = pl.program_id(0)
    d_exp = wdn_ref.shape[1]
    prev = jnp.maximum(i - 1, 0)
    changed = (i == 0) | (be_ref[i] != be_ref[prev])

    @pl.when(changed)
    def _():
        wgu_bf[...] = wgu_ref[0].astype(_BF16)
        wdn_bf[...] = wdn_ref[0].astype(_BF16)

    @pl.when(i >= nv_ref[0])
    def _():
        ys_ref[...] = jnp.zeros_like(ys_ref)

    @pl.when(i < nv_ref[0])
    def _():
        xb = xs_ref[...].astype(_BF16)
        gu = _dot(xb, wgu_bf[...]) + bgu_ref[0]
        gate = jnp.minimum(gu[:, 0:d_exp], SWIGLU_LIMIT)
        up = jnp.clip(gu[:, d_exp:], -SWIGLU_LIMIT, SWIGLU_LIMIT)
        act = (up + 1.0) * (gate * _sigmoid(gate * SWIGLU_ALPHA))
        ys_ref[...] = _dot(act.astype(_BF16), wdn_bf[...]) + bdn_ref[0]


def _experts_call(blk_e, n_valid, xs, wgu, bgu, wdn, bdn, bm, n_blocks):
    d = xs.shape[1]
    n_slots = n_blocks * bm
    d_gu = wgu.shape[2]
    d_exp = wdn.shape[1]
    return pl.pallas_call(
        _experts_kernel,
        grid_spec=pltpu.PrefetchScalarGridSpec(
            num_scalar_prefetch=2,
            grid=(n_blocks,),
            in_specs=[pl.BlockSpec((bm, d), lambda i, be, nv: (i, 0)),
                      pl.BlockSpec((1, d, d_gu), lambda i, be, nv: (be[i], 0, 0)),
                      pl.BlockSpec((1, 1, d_gu), lambda i, be, nv: (be[i], 0, 0)),
                      pl.BlockSpec((1, d_exp, d), lambda i, be, nv: (be[i], 0, 0)),
                      pl.BlockSpec((1, 1, d), lambda i, be, nv: (be[i], 0, 0))],
            out_specs=pl.BlockSpec((bm, d), lambda i, be, nv: (i, 0)),
            scratch_shapes=[pltpu.VMEM((d, d_gu), _BF16), pltpu.VMEM((d_exp, d), _BF16)]),
        out_shape=jax.ShapeDtypeStruct((n_slots, d), _F32),
        compiler_params=pltpu.CompilerParams(
            dimension_semantics=("arbitrary",), vmem_limit_bytes=VMEM_LIMIT),
        name="experts",
    )(blk_e, n_valid, xs, wgu, bgu, wdn, bdn)


def _combine_kernel(src_ref, nxt_ref, slot_ref, gate_ref, h1_ref, gfin_ref, ys_ref, out_ref,
                    ybuf, sem):
    i = pl.program_id(0)
    n_sub, ns, _ = ybuf.shape
    ts = h1_ref.shape[0] // n_sub

    def gather(tab_ref, h_from, h_to):
        def issue(q, carry):
            at = pl.multiple_of(tab_ref[0, h_from, q], SUBLANES)
            to = pl.multiple_of(q * SUBLANES, SUBLANES)
            pltpu.make_async_copy(ys_ref.at[pl.ds(at, SUBLANES)],
                                  ybuf.at[h_to, pl.ds(to, SUBLANES)], sem.at[h_to]).start()
            return carry

        lax.fori_loop(0, ns // SUBLANES, issue, 0)

    @pl.when(i == 0)
    def _():
        gather(src_ref, 0, 0)

    for h in range(n_sub):
        if h + 1 < n_sub:
            gather(src_ref, h + 1, h + 1)
        pltpu.make_async_copy(ys_ref.at[pl.ds(0, ns)], ybuf.at[h], sem.at[h]).wait()

        slot = slot_ref[h * ts:(h + 1) * ts, :]
        gate = gate_ref[h * ts:(h + 1) * ts, :]
        cols = lax.broadcasted_iota(jnp.int32, (ts, ns), 1)
        g = jnp.where(cols == slot[:, 0:1], gate[:, 0:1], 0.0)
        for k in range(1, TOP_K):
            g = g + jnp.where(cols == slot[:, k:k + 1], gate[:, k:k + 1], 0.0)
        g_hi = g.astype(_BF16)
        g_lo = (g - g_hi.astype(_F32)).astype(_BF16)
        yb = ybuf[h].astype(_BF16)
        if h + 1 == n_sub:
            @pl.when(i < pl.num_programs(0) - 1)
            def _():
                gather(nxt_ref, 0, 0)
        acc = h1_ref[h * ts:(h + 1) * ts, :] + _dot(g_hi, yb) + _dot(g_lo, yb)
        out_ref[h * ts:(h + 1) * ts, :] = _rms(acc, gfin_ref[...])


def _combine_call(src, slot_t, gate_t, h1, gfin, ys, tq, ns):
    n_tok, d = h1.shape
    n_steps, n_sub, nq = src.shape
    table = lambda f: pl.BlockSpec((1, n_sub, nq), f, memory_space=pltpu.SMEM)
    return pl.pallas_call(
        _combine_kernel,
        grid=(n_steps,),
        in_specs=[table(lambda i: (i, 0, 0)),
                  table(lambda i: (jnp.minimum(i + 1, n_steps - 1), 0, 0)),
                  pl.BlockSpec((tq, TOP_K), lambda i: (i, 0)),
                  pl.BlockSpec((tq, TOP_K), lambda i: (i, 0)),
                  pl.BlockSpec((tq, d), lambda i: (i, 0)),
                  pl.BlockSpec((1, d), lambda i: (0, 0)),
                  pl.BlockSpec(memory_space=pl.ANY)],
        out_specs=pl.BlockSpec((tq, d), lambda i: (i, 0)),
        out_shape=jax.ShapeDtypeStruct((n_tok, d), _F32),
        scratch_shapes=[pltpu.VMEM((n_sub, ns, d), _F32), pltpu.SemaphoreType.DMA((n_sub,))],
        compiler_params=pltpu.CompilerParams(
            dimension_semantics=("arbitrary",), vmem_limit_bytes=VMEM_LIMIT),
        name="combine",
    )(src, src, slot_t, gate_t, h1, gfin, ys)


def kernel(x, meta_tokens, norm_mix_g, w_in, b_in, w_pool_grp, pool_scale, w_dwconv, b_dwconv, conv_ln_g, conv_ln_b, w_pw2, b_pw2, w_out, norm_ffn_g, router_w, router_b, w_gate_up, b_gate_up, w_down, b_down, norm_final_g):
    bsz, seq, d = x.shape
    assert w_in.shape[0] == 1, "one layer"
    tq = min(MIX_TILE, seq)
    ts = min(SORT_TILE, tq)
    n_sub = tq // ts
    bm = EXPERT_BLOCK
    n_tok = bsz * seq
    n_sort = n_tok // ts
    ns = -(-(ts * TOP_K + N_EXPERTS * (SUBLANES - 1)) // LANES) * LANES
    nq = ns // SUBLANES
    n_main = (n_tok * TOP_K + n_sort * N_EXPERTS * (SUBLANES - 1)) // bm + N_EXPERTS
    n_spill = -(-(n_sub * ns) // bm)
    i32 = jnp.int32

    row = lambda a: a.reshape(1, -1)
    wr_t = router_w[0].T
    wr_hi = wr_t.astype(_BF16)
    wr_lo = (wr_t - wr_hi.astype(_F32)).astype(_BF16)
    tri = (jnp.arange(ts)[:, None] < jnp.arange(ts)[None, :]).astype(_BF16)
    lmat = (jnp.arange(N_EXPERTS)[None, :] < jnp.arange(N_EXPERTS)[:, None]).astype(_BF16)

    h1, hn, gate, slot, segend, off, tot = _mixer_call(
        x, meta_tokens, row(norm_mix_g[0]), w_in[0].astype(_BF16), row(b_in[0]),
        w_pool_grp[0].astype(_BF16), row(pool_scale[0]), w_dwconv[0], row(b_dwconv[0]),
        row(conv_ln_g[0]), row(conv_ln_b[0]), w_pw2[0].astype(_BF16), row(b_pw2[0]),
        w_out[0].astype(_BF16), row(norm_ffn_g[0]), jnp.concatenate([wr_hi, wr_lo], axis=0),
        router_b[0].reshape(N_EXPERTS, 1), tri, lmat, tq)

    segend = segend[:, :, 0].astype(i32)
    off = off[:, :, 0].astype(i32)
    total = tot[:, 0].astype(i32)
    padded = (total + bm - 1) // bm * bm
    pend = jnp.cumsum(padded)
    pstart = pend - padded
    n_valid = pend[-1] // bm
    blk = jnp.minimum(jnp.arange(n_main, dtype=i32), n_valid - 1)
    blk_e = jnp.sum((pend[None, :] <= (blk * bm)[:, None]).astype(i32), axis=1)
    blk_e = jnp.minimum(blk_e, N_EXPERTS - 1)
    fill = jnp.stack([(pstart + total) // SUBLANES, pend // SUBLANES,
                      jnp.full_like(pend, n_valid)]).astype(i32)

    q_row = jnp.arange(nq, dtype=i32) * SUBLANES
    e_q = jnp.sum((segend[:, None, :] <= q_row[None, :, None]).astype(i32), axis=-1)
    used = q_row[None, :] < segend[:, -1:]
    hot = jnp.minimum(e_q, N_EXPERTS - 1)[..., None] == jnp.arange(N_EXPERTS, dtype=i32)
    to_row = jnp.sum(jnp.where(hot, (pstart[None, :] + off)[:, None, :], 0), axis=-1) + q_row
    spill = n_main * bm + (jnp.arange(n_sort, dtype=i32) % n_sub)[:, None] * ns + q_row
    dst = jnp.where(used, to_row, spill).reshape(n_sort // n_sub, n_sub, nq)
    src = jnp.where(used, to_row, 0).reshape(n_sort // n_sub, n_sub, nq)

    slot_t = slot.transpose(0, 2, 1).reshape(n_tok, TOP_K)
    gate_t = gate.transpose(0, 2, 1).reshape(n_tok, TOP_K)

    xs = _dispatch_call(fill, dst, slot, hn, (n_main + n_spill) * bm, tq, ns, bm)
    ys = _experts_call(blk_e, n_valid.reshape(1).astype(i32), xs,
                       w_gate_up[0], b_gate_up[0].reshape(N_EXPERTS, 1, -1),
                       w_down[0], b_down[0].reshape(N_EXPERTS, 1, -1), bm, n_main)
    out = _combine_call(src, slot_t, gate_t, h1, row(norm_final_g), ys, tq, ns)
    return out.reshape(bsz, seq, d)
```

```python
import functools

import jax
import jax.numpy as jnp
from jax import lax
from jax.experimental import pallas as pl
from jax.experimental.pallas import tpu as pltpu

N_META = 16
POOL_WINDOWS = (2, 4, 8, 16)
POOL_GROUP_DIM = 128
POOL_OUT_DIM = 256
POOL_WIDTH = POOL_GROUP_DIM * len(POOL_WINDOWS)
CONV_WIDTH = 512
CONV_KERNEL = 31
N_EXPERTS = 32
TOP_K = 4
SWIGLU_LIMIT = 7.0
SWIGLU_ALPHA = 1.702
RMS_EPS = 1e-5
LN_EPS = 1e-5

LANES = 128
SUBLANES = 8

MIX_TILE = 512
POOL_HALO = 16
CONV_HALO = 32
CONV_ROWS = 64
SORT_TILE = 256
EXPERT_BLOCK = 256
VMEM_LIMIT = 56 * 1024 * 1024

_F32 = jnp.float32
_BF16 = jnp.bfloat16


def _rms(x, g):
    return x * lax.rsqrt(jnp.mean(x * x, axis=-1, keepdims=True) + RMS_EPS) * g


def _sigmoid(x):
    return 1.0 / (1.0 + jnp.exp(-x))


def _dot(a, b):
    return jnp.dot(a, b, preferred_element_type=_F32)


def _mixer_kernel(x_ref, meta_ref, gmix_ref, win_ref, bin_ref, wpool_ref, pscale_ref,
                  kdw_ref, bdw_ref, lng_ref, lnb_ref, wpw2_ref, bpw2_ref, wout_ref,
                  gffn_ref, wr_ref, rb_ref, tri_ref, lmat_ref,
                  h1_ref, hn_ref, gate_ref, slot_ref, segend_ref, off_ref, tot_ref,
                  ubuf, abuf, base):
    b = pl.program_id(0)
    j = pl.program_id(1)
    tq = x_ref.shape[1]
    c0 = POOL_WIDTH
    c1 = c0 + CONV_WIDTH
    c2 = c1 + CONV_WIDTH
    d_model = x_ref.shape[2]
    c3 = c2 + d_model
    n_slab = CONV_WIDTH // LANES

    @pl.when((b == 0) & (j == 0))
    def _():
        base[...] = jnp.zeros_like(base)

    @pl.when(j == 0)
    def _():
        xm = _rms(meta_ref[...], gmix_ref[...]).astype(_BF16)
        pm = _dot(xm, win_ref[:, 0:c2]) + bin_ref[:, 0:c2]
        um = pm[:, 0:c0]
        am = pm[:, c0:c1] * _sigmoid(pm[:, c1:c2])
        for s in range(n_slab):
            ubuf[s, 0:POOL_HALO, :] = um[:, s * LANES:(s + 1) * LANES]
            abuf[s, 0:CONV_HALO - N_META, :] = jnp.zeros((CONV_HALO - N_META, LANES), _F32)
            abuf[s, CONV_HALO - N_META:CONV_HALO, :] = am[:, s * LANES:(s + 1) * LANES]

    ts = tri_ref.shape[0]
    n_sub = tq // ts
    nt_dims = (((1,), (1,)), ((), ()))
    eidx = lax.broadcasted_iota(jnp.int32, (N_EXPERTS, ts), 0).astype(_F32)

    def proj_stage(h):
        r0 = h * ts
        xn = _rms(x_ref[0, r0:r0 + ts, :], gmix_ref[...]).astype(_BF16)
        u = _dot(xn, win_ref[:, 0:c0]) + bin_ref[:, 0:c0]
        v = _dot(xn, win_ref[:, c0:c1]) + bin_ref[:, c0:c1]
        gt = _dot(xn, win_ref[:, c1:c2]) + bin_ref[:, c1:c2]
        a = v * _sigmoid(gt)
        for s in range(n_slab):
            ubuf[s, POOL_HALO + r0:POOL_HALO + r0 + ts, :] = u[:, s * LANES:(s + 1) * LANES]
            abuf[s, CONV_HALO + r0:CONV_HALO + r0 + ts, :] = a[:, s * LANES:(s + 1) * LANES]
        return xn

    def conv_stage(h):
        r0 = h * ts
        conv = []
        for s in range(n_slab):
            lo = s * LANES
            rows = []
            for rc in range(r0, r0 + ts, CONV_ROWS):
                acc = jnp.broadcast_to(bdw_ref[:, lo:lo + LANES], (CONV_ROWS, LANES))
                for tap in range(CONV_KERNEL):
                    start = CONV_HALO - (CONV_KERNEL - 1) + tap + rc
                    acc = acc + kdw_ref[tap:tap + 1, lo:lo + LANES] * abuf[s, start:start + CONV_ROWS, :]
                rows.append(acc)
            conv.append(jnp.concatenate(rows, axis=0))
        return jnp.concatenate(conv, axis=-1)

    def mix_stage(h, c, xn):
        r0 = h * ts
        g_a = _dot(xn, win_ref[:, c2:c3]) + bin_ref[:, c2:c3]
        g_b = _dot(xn, win_ref[:, c3:]) + bin_ref[:, c3:]
        ya = []
        for g, w in enumerate(POOL_WINDOWS):
            ug = ubuf[g, POOL_HALO + r0:POOL_HALO + r0 + ts, :]
            acc = ug
            for back in range(1, w):
                acc = acc + ubuf[g, POOL_HALO + r0 - back:POOL_HALO + r0 - back + ts, :]
            dg = acc * (1.0 / w) - ug
            ya.append(_dot(dg.astype(_BF16), wpool_ref[g]))
        y_a = jnp.concatenate(ya, axis=-1) * pscale_ref[...]

        mu = jnp.mean(c, axis=-1, keepdims=True)
        cc = c - mu
        var = jnp.mean(cc * cc, axis=-1, keepdims=True)
        cn = cc * lax.rsqrt(var + LN_EPS) * lng_ref[...] + lnb_ref[...]
        sw = cn * _sigmoid(cn)
        y_b = _dot(sw.astype(_BF16), wpw2_ref[...]) + bpw2_ref[...]

        mixed = _sigmoid(g_a) * y_a + _sigmoid(g_b) * y_b
        h1 = x_ref[0, r0:r0 + ts, :] + _dot(mixed.astype(_BF16), wout_ref[...])
        h1_ref[r0:r0 + ts, :] = h1
        hn_ref[r0:r0 + ts, :] = _rms(h1, gffn_ref[...])

    def route_stage(h):
        r0 = h * ts
        hn = hn_ref[r0:r0 + ts, :]
        hn_hi = hn.astype(_BF16)
        hn_lo = (hn - hn_hi.astype(_F32)).astype(_BF16)
        l_hi = lax.dot_general(wr_ref[...], hn_hi, nt_dims, preferred_element_type=_F32)
        l_lo = lax.dot_general(wr_ref[0:N_EXPERTS, :], hn_lo, nt_dims, preferred_element_type=_F32)
        work = l_hi[0:N_EXPERTS] + l_hi[N_EXPERTS:2 * N_EXPERTS] + l_lo + rb_ref[...]

        vals, hots = [], []
        for _ in range(TOP_K):
            m = jnp.max(work, axis=0, keepdims=True)
            first = jnp.min(jnp.where(work == m, eidx, float(N_EXPERTS)), axis=0, keepdims=True)
            hot = eidx == first
            vals.append(m)
            hots.append(hot)
            work = jnp.where(hot, -jnp.inf, work)
        exps = [jnp.exp(vk - vals[0]) for vk in vals]
        denom = exps[0] + exps[1] + exps[2] + exps[3]
        gate_ref[0, :, r0:r0 + ts] = jnp.concatenate([ek / denom for ek in exps], axis=0)

        chosen = jnp.where(hots[0] | hots[1] | hots[2] | hots[3], 1.0, 0.0)
        groups = jnp.floor((jnp.sum(chosen, axis=1, keepdims=True) + (SUBLANES - 1)) * (1.0 / SUBLANES))
        groups = jnp.broadcast_to(groups, (N_EXPERTS, LANES))
        seg = _dot(lmat_ref[...], groups.astype(_BF16)) * float(SUBLANES)
        place = seg[:, 0:1] + _dot(chosen.astype(_BF16), tri_ref[...])
        slots = [jnp.sum(jnp.where(m, place, 0.0), axis=0, keepdims=True) for m in hots]
        slot_ref[h] = jnp.concatenate(slots, axis=0).astype(jnp.int32)
        segend_ref[h] = seg + groups * float(SUBLANES)
        off_ref[h] = base[...] - seg
        base[...] = base[...] + groups * float(SUBLANES)

    xns = [proj_stage(h) for h in range(n_sub)]
    for h in range(n_sub):
        mix_stage(h, conv_stage(h), xns[h])
        if h > 0:
            route_stage(h - 1)
    route_stage(n_sub - 1)
    tot_ref[...] = base[...]

    for s in range(n_slab):
        ubuf[s, 0:POOL_HALO, :] = ubuf[s, tq:tq + POOL_HALO, :]
        abuf[s, 0:CONV_HALO, :] = abuf[s, tq:tq + CONV_HALO, :]


def _mixer_call(x, meta, gmix, win, bin_, wpool, pscale, kdw, bdw, lng, lnb, wpw2, bpw2,
                wout, gffn, wr, rb, tri, lmat, tq):
    bsz, seq, d = x.shape
    nj = seq // tq
    n_tok = bsz * seq
    n_tiles = bsz * nj
    ts = tri.shape[0]
    n_sub = tq // ts

    def full(arr):
        nd = arr.ndim
        return pl.BlockSpec(arr.shape, lambda b, j, _n=nd: (0,) * _n)

    def tile3(k, w):
        return pl.BlockSpec((k, TOP_K, w), lambda b, j: (b * nj + j, 0, 0))

    per_expert = pl.BlockSpec((n_sub, N_EXPERTS, LANES), lambda b, j: (b * nj + j, 0, 0))
    rows = pl.BlockSpec((tq, d), lambda b, j: (b * nj + j, 0))
    consts = (meta, gmix, win, bin_, wpool, pscale, kdw, bdw, lng, lnb, wpw2, bpw2, wout,
              gffn, wr, rb, tri, lmat)
    return pl.pallas_call(
        _mixer_kernel,
        grid=(bsz, nj),
        in_specs=[pl.BlockSpec((1, tq, d), lambda b, j: (b, j, 0))] + [full(c) for c in consts],
        out_specs=[rows, rows, tile3(1, tq), tile3(n_sub, ts), per_expert, per_expert,
                   pl.BlockSpec((N_EXPERTS, LANES), lambda b, j: (0, 0))],
        out_shape=[jax.ShapeDtypeStruct((n_tok, d), _F32),
                   jax.ShapeDtypeStruct((n_tok, d), _F32),
                   jax.ShapeDtypeStruct((n_tiles, TOP_K, tq), _F32),
                   jax.ShapeDtypeStruct((n_tiles * n_sub, TOP_K, ts), jnp.int32),
                   jax.ShapeDtypeStruct((n_tiles * n_sub, N_EXPERTS, LANES), _F32),
                   jax.ShapeDtypeStruct((n_tiles * n_sub, N_EXPERTS, LANES), _F32),
                   jax.ShapeDtypeStruct((N_EXPERTS, LANES), _F32)],
        scratch_shapes=[pltpu.VMEM((CONV_WIDTH // LANES, POOL_HALO + tq, LANES), _F32),
                        pltpu.VMEM((CONV_WIDTH // LANES, CONV_HALO + tq, LANES), _F32),
                        pltpu.VMEM((N_EXPERTS, LANES), _F32)],
        compiler_params=pltpu.CompilerParams(
            dimension_semantics=("arbitrary", "arbitrary"), vmem_limit_bytes=VMEM_LIMIT),
        name="mixer",
    )(x, *consts)


def _dispatch_kernel(fill_ref, dst_ref, slot_ref, hn_ref, xs_ref, sbuf, zblk, sem, zsem, *, bm):
    i = pl.program_id(0)
    n_sub, _, ts = slot_ref.shape
    ns = sbuf.shape[1]
    n_blocks = xs_ref.shape[0] // bm

    def zero_fill(lo, hi, rows):
        def start(q, carry):
            at = pl.multiple_of(q * rows, rows)
            pltpu.make_async_copy(zblk.at[pl.ds(0, rows)], xs_ref.at[pl.ds(at, rows)], zsem).start()
            return carry

        def drain(q, carry):
            at = pl.multiple_of(q * rows, rows)
            pltpu.make_async_copy(zblk.at[pl.ds(0, rows)], xs_ref.at[pl.ds(at, rows)], zsem).wait()
            return carry

        lax.fori_loop(lo, hi, start, 0)
        lax.fori_loop(lo, hi, drain, 0)

    @pl.when(i == 0)
    def _():
        zblk[...] = jnp.zeros_like(zblk)
        for e in range(N_EXPERTS):
            zero_fill(fill_ref[0, e], fill_ref[1, e], SUBLANES)
        zero_fill(fill_ref[2, 0], n_blocks, bm)

    for h in range(n_sub):
        @pl.when(i > 0)
        def _():
            pltpu.make_async_copy(sbuf.at[h], xs_ref.at[pl.ds(0, ns)], sem.at[h]).wait()

        slot = slot_ref[h]
        rows = lax.broadcasted_iota(jnp.int32, (ns, ts), 0)
        sel = jnp.where(rows == slot[0:1, :], 1.0, 0.0)
        for k in range(1, TOP_K):
            sel = sel + jnp.where(rows == slot[k:k + 1, :], 1.0, 0.0)
        sbuf[h] = _dot(sel.astype(_BF16), hn_ref[h * ts:(h + 1) * ts, :].astype(_BF16))

        def issue(q, carry):
            at = pl.multiple_of(q * SUBLANES, SUBLANES)
            to = pl.multiple_of(dst_ref[0, h, q], SUBLANES)
            pltpu.make_async_copy(sbuf.at[h, pl.ds(at, SUBLANES)],
                                  xs_ref.at[pl.ds(to, SUBLANES)], sem.at[h]).start()
            return carry

        lax.fori_loop(0, ns // SUBLANES, issue, 0)

    @pl.when(i == pl.num_programs(0) - 1)
    def _():
        for h in range(n_sub):
            pltpu.make_async_copy(sbuf.at[h], xs_ref.at[pl.ds(0, ns)], sem.at[h]).wait()


def _dispatch_call(fill, dst, slot, hn, n_rows, tq, ns, bm):
    n_tok, d = hn.shape
    n_sub, ts = dst.shape[1], slot.shape[2]
    return pl.pallas_call(
        functools.partial(_dispatch_kernel, bm=bm),
        grid_spec=pltpu.PrefetchScalarGridSpec(
            num_scalar_prefetch=1,
            grid=(n_tok // tq,),
            in_specs=[pl.BlockSpec((1, n_sub, ns // SUBLANES), lambda i, f: (i, 0, 0),
                                   memory_space=pltpu.SMEM),
                      pl.BlockSpec((n_sub, TOP_K, ts), lambda i, f: (i, 0, 0)),
                      pl.BlockSpec((tq, d), lambda i, f: (i, 0))],
            out_specs=pl.BlockSpec(memory_space=pl.ANY),
            scratch_shapes=[pltpu.VMEM((n_sub, ns, d), _F32), pltpu.VMEM((bm, d), _F32),
                            pltpu.SemaphoreType.DMA((n_sub,)), pltpu.SemaphoreType.DMA]),
        out_shape=jax.ShapeDtypeStruct((n_rows, d), _F32),
        compiler_params=pltpu.CompilerParams(
            dimension_semantics=("arbitrary",), vmem_limit_bytes=VMEM_LIMIT),
        name="dispatch",
    )(fill, dst, slot, hn)


def _experts_kernel(nb_ref, r0_ref, nv_ref, xs_ref, wgu_ref, bgu_ref, wdn_ref, bdn_ref, ys_ref,
                    wgu_bf, wdn_bf, xbuf, ybuf, xsem, ysem, *, n_blocks):
    e = pl.program_id(0)
    n_exp = pl.num_programs(0)
    bm = xbuf.shape[1]
    d_exp = wdn_ref.shape[1]
    nb = nb_ref[e]

    def x_copy(ex, j, slot):
        at = pl.multiple_of(r0_ref[ex] + j * bm, bm)
        return pltpu.make_async_copy(xs_ref.at[pl.ds(at, bm)], xbuf.at[slot], xsem.at[slot])

    def y_copy(j, slot):
        at = pl.multiple_of(r0_ref[e] + j * bm, bm)
        return pltpu.make_async_copy(ybuf.at[slot], ys_ref.at[pl.ds(at, bm)], ysem.at[slot])

    @pl.when((e == 0) & (nb > 0))
    def _():
        x_copy(e, 0, 0).start()

    @pl.when(nb > 0)
    def _():
        wgu_bf[...] = wgu_ref[0].astype(_BF16)
        wdn_bf[...] = wdn_ref[0].astype(_BF16)

    def block(j, carry):
        slot = lax.rem(j, 2)
        x_copy(e, j, slot).wait()

        @pl.when(j + 1 < nb)
        def _():
            x_copy(e, j + 1, 1 - slot).start()

        @pl.when(j >= 2)
        def _():
            y_copy(j - 2, slot).wait()

        xb = xbuf[slot].astype(_BF16)
        gu = _dot(xb, wgu_bf[...]) + bgu_ref[0]
        gate = jnp.minimum(gu[:, 0:d_exp], SWIGLU_LIMIT)
        up = jnp.clip(gu[:, d_exp:], -SWIGLU_LIMIT, SWIGLU_LIMIT)
        act = (up + 1.0) * (gate * _sigmoid(gate * SWIGLU_ALPHA))
        ybuf[slot] = _dot(act.astype(_BF16), wdn_bf[...]) + bdn_ref[0]
        y_copy(j, slot).start()
        return carry

    lax.fori_loop(0, nb, block, 0)

    nxt = jnp.minimum(e + 1, n_exp - 1)

    @pl.when((e + 1 < n_exp) & (nb_ref[nxt] > 0))
    def _():
        x_copy(nxt, 0, 0).start()

    @pl.when(nb >= 2)
    def _():
        y_copy(nb - 2, lax.rem(nb, 2)).wait()

    @pl.when(nb >= 1)
    def _():
        y_copy(nb - 1, lax.rem(nb + 1, 2)).wait()

    @pl.when(e == n_exp - 1)
    def _():
        ybuf[0] = jnp.zeros(ybuf.shape[1:], _F32)

        def start(q, carry):
            at = pl.multiple_of(q * bm, bm)
            pltpu.make_async_copy(ybuf.at[0], ys_ref.at[pl.ds(at, bm)], ysem.at[0]).start()
            return carry

        def drain(q, carry):
            at = pl.multiple_of(q * bm, bm)
            pltpu.make_async_copy(ybuf.at[0], ys_ref.at[pl.ds(at, bm)], ysem.at[0]).wait()
            return carry

        lax.fori_loop(nv_ref[0], n_blocks, start, 0)
        lax.fori_loop(nv_ref[0], n_blocks, drain, 0)


def _experts_call(nb, r0, n_valid, xs, wgu, bgu, wdn, bdn, bm, n_blocks):
    d = xs.shape[1]
    n_exp, _, d_gu = wgu.shape
    d_exp = wdn.shape[1]
    return pl.pallas_call(
        functools.partial(_experts_kernel, n_blocks=n_blocks),
        grid_spec=pltpu.PrefetchScalarGridSpec(
            num_scalar_prefetch=3,
            grid=(n_exp,),
            in_specs=[pl.BlockSpec(memory_space=pl.ANY),
                      pl.BlockSpec((1, d, d_gu), lambda e, *_: (e, 0, 0)),
                      pl.BlockSpec((1, 1, d_gu), lambda e, *_: (e, 0, 0)),
                      pl.BlockSpec((1, d_exp, d), lambda e, *_: (e, 0, 0)),
                      pl.BlockSpec((1, 1, d), lambda e, *_: (e, 0, 0))],
            out_specs=pl.BlockSpec(memory_space=pl.ANY),
            scratch_shapes=[pltpu.VMEM((d, d_gu), _BF16), pltpu.VMEM((d_exp, d), _BF16),
                            pltpu.VMEM((2, bm, d), _F32), pltpu.VMEM((2, bm, d), _F32),
                            pltpu.SemaphoreType.DMA((2,)), pltpu.SemaphoreType.DMA((2,))]),
        out_shape=jax.ShapeDtypeStruct((n_blocks * bm, d), _F32),
        compiler_params=pltpu.CompilerParams(
            dimension_semantics=("arbitrary",), vmem_limit_bytes=VMEM_LIMIT),
        name="experts",
    )(nb, r0, n_valid, xs, wgu, bgu, wdn, bdn)


def _combine_kernel(src_ref, nxt_ref, slot_ref, gate_ref, h1_ref, gfin_ref, ys_ref, out_ref,
                    ybuf, sem):
    i = pl.program_id(0)
    n_sub, ns, _ = ybuf.shape
    ts = h1_ref.shape[0] // n_sub

    def gather(tab_ref, h_from, h_to):
        def issue(q, carry):
            at = pl.multiple_of(tab_ref[0, h_from, q], SUBLANES)
            to = pl.multiple_of(q * SUBLANES, SUBLANES)
            pltpu.make_async_copy(ys_ref.at[pl.ds(at, SUBLANES)],
                                  ybuf.at[h_to, pl.ds(to, SUBLANES)], sem.at[h_to]).start()
            return carry

        lax.fori_loop(0, ns // SUBLANES, issue, 0)

    @pl.when(i == 0)
    def _():
        gather(src_ref, 0, 0)

    for h in range(n_sub):
        if h + 1 < n_sub:
            gather(src_ref, h + 1, h + 1)
        pltpu.make_async_copy(ys_ref.at[pl.ds(0, ns)], ybuf.at[h], sem.at[h]).wait()

        slot = slot_ref[h * ts:(h + 1) * ts, :]
        gate = gate_ref[h * ts:(h + 1) * ts, :]
        cols = lax.broadcasted_iota(jnp.int32, (ts, ns), 1)
        g = jnp.where(cols == slot[:, 0:1], gate[:, 0:1], 0.0)
        for k in range(1, TOP_K):
            g = g + jnp.where(cols == slot[:, k:k + 1], gate[:, k:k + 1], 0.0)
        g_hi = g.astype(_BF16)
        g_lo = (g - g_hi.astype(_F32)).astype(_BF16)
        yb = ybuf[h].astype(_BF16)
        if h + 1 == n_sub:
            @pl.when(i < pl.num_programs(0) - 1)
            def _():
                gather(nxt_ref, 0, 0)
        acc = h1_ref[h * ts:(h + 1) * ts, :] + _dot(g_hi, yb) + _dot(g_lo, yb)
        out_ref[h * ts:(h + 1) * ts, :] = _rms(acc, gfin_ref[...])


def _combine_call(src, slot_t, gate_t, h1, gfin, ys, tq, ns):
    n_tok, d = h1.shape
    n_steps, n_sub, nq = src.shape
    table = lambda f: pl.BlockSpec((1, n_sub, nq), f, memory_space=pltpu.SMEM)
    return pl.pallas_call(
        _combine_kernel,
        grid=(n_steps,),
        in_specs=[table(lambda i: (i, 0, 0)),
                  table(lambda i: (jnp.minimum(i + 1, n_steps - 1), 0, 0)),
                  pl.BlockSpec((tq, TOP_K), lambda i: (i, 0)),
                  pl.BlockSpec((tq, TOP_K), lambda i: (i, 0)),
                  pl.BlockSpec((tq, d), lambda i: (i, 0)),
                  pl.BlockSpec((1, d), lambda i: (0, 0)),
                  pl.BlockSpec(memory_space=pl.ANY)],
        out_specs=pl.BlockSpec((tq, d), lambda i: (i, 0)),
        out_shape=jax.ShapeDtypeStruct((n_tok, d), _F32),
        scratch_shapes=[pltpu.VMEM((n_sub, ns, d), _F32), pltpu.SemaphoreType.DMA((n_sub,))],
        compiler_params=pltpu.CompilerParams(
            dimension_semantics=("arbitrary",), vmem_limit_bytes=VMEM_LIMIT),
        name="combine",
    )(src, src, slot_t, gate_t, h1, gfin, ys)


def kernel(x, meta_tokens, norm_mix_g, w_in, b_in, w_pool_grp, pool_scale, w_dwconv, b_dwconv, conv_ln_g, conv_ln_b, w_pw2, b_pw2, w_out, norm_ffn_g, router_w, router_b, w_gate_up, b_gate_up, w_down, b_down, norm_final_g):
    bsz, seq, d = x.shape
    assert w_in.shape[0] == 1, "one layer"
    tq = min(MIX_TILE, seq)
    ts = min(SORT_TILE, tq)
    n_sub = tq // ts
    bm = EXPERT_BLOCK
    n_tok = bsz * seq
    n_sort = n_tok // ts
    ns = -(-(ts * TOP_K + N_EXPERTS * (SUBLANES - 1)) // LANES) * LANES
    nq = ns // SUBLANES
    n_main = (n_tok * TOP_K + n_sort * N_EXPERTS * (SUBLANES - 1)) // bm + N_EXPERTS
    n_spill = -(-(n_sub * ns) // bm)
    i32 = jnp.int32

    row = lambda a: a.reshape(1, -1)
    wr_t = router_w[0].T
    wr_hi = wr_t.astype(_BF16)
    wr_lo = (wr_t - wr_hi.astype(_F32)).astype(_BF16)
    tri = (jnp.arange(ts)[:, None] < jnp.arange(ts)[None, :]).astype(_BF16)
    lmat = (jnp.arange(N_EXPERTS)[None, :] < jnp.arange(N_EXPERTS)[:, None]).astype(_BF16)

    h1, hn, gate, slot, segend, off, tot = _mixer_call(
        x, meta_tokens, row(norm_mix_g[0]), w_in[0].astype(_BF16), row(b_in[0]),
        w_pool_grp[0].astype(_BF16), row(pool_scale[0]), w_dwconv[0], row(b_dwconv[0]),
        row(conv_ln_g[0]), row(conv_ln_b[0]), w_pw2[0].astype(_BF16), row(b_pw2[0]),
        w_out[0].astype(_BF16), row(norm_ffn_g[0]), jnp.concatenate([wr_hi, wr_lo], axis=0),
        router_b[0].reshape(N_EXPERTS, 1), tri, lmat, tq)

    segend = segend[:, :, 0].astype(i32)
    off = off[:, :, 0].astype(i32)
    total = tot[:, 0].astype(i32)
    padded = (total + bm - 1) // bm * bm
    pend = jnp.cumsum(padded)
    pstart = pend - padded
    n_valid = pend[-1] // bm
    fill = jnp.stack([(pstart + total) // SUBLANES, pend // SUBLANES,
                      jnp.full_like(pend, n_valid)]).astype(i32)

    q_row = jnp.arange(nq, dtype=i32) * SUBLANES
    e_q = jnp.sum((segend[:, None, :] <= q_row[None, :, None]).astype(i32), axis=-1)
    used = q_row[None, :] < segend[:, -1:]
    hot = jnp.minimum(e_q, N_EXPERTS - 1)[..., None] == jnp.arange(N_EXPERTS, dtype=i32)
    to_row = jnp.sum(jnp.where(hot, (pstart[None, :] + off)[:, None, :], 0), axis=-1) + q_row
    spill = n_main * bm + (jnp.arange(n_sort, dtype=i32) % n_sub)[:, None] * ns + q_row
    dst = jnp.where(used, to_row, spill).reshape(n_sort // n_sub, n_sub, nq)
    src = jnp.where(used, to_row, 0).reshape(n_sort // n_sub, n_sub, nq)

    slot_t = slot.transpose(0, 2, 1).reshape(n_tok, TOP_K)
    gate_t = gate.transpose(0, 2, 1).reshape(n_tok, TOP_K)

    xs = _dispatch_call(fill, dst, slot, hn, (n_main + n_spill) * bm, tq, ns, bm)
    ys = _experts_call((padded // bm).astype(i32), pstart.astype(i32),
                       n_valid.reshape(1).astype(i32), xs,
                       w_gate_up[0], b_gate_up[0].reshape(N_EXPERTS, 1, -1),
                       w_down[0], b_down[0].reshape(N_EXPERTS, 1, -1), bm, n_main)
    out = _combine_call(src, slot_t, gate_t, h1, row(norm_final_g), ys, tq, ns)
    return out.reshape(bsz, seq, d)
```

```python
import functools

import jax
import jax.numpy as jnp
from jax import lax
from jax.experimental import pallas as pl
from jax.experimental.pallas import tpu as pltpu

N_META = 16
POOL_WINDOWS = (2, 4, 8, 16)
POOL_GROUP_DIM = 128
POOL_OUT_DIM = 256
POOL_WIDTH = POOL_GROUP_DIM * len(POOL_WINDOWS)
CONV_WIDTH = 512
CONV_KERNEL = 31
N_EXPERTS = 32
TOP_K = 4
SWIGLU_LIMIT = 7.0
SWIGLU_ALPHA = 1.702
RMS_EPS = 1e-5
LN_EPS = 1e-5

LANES = 128
SUBLANES = 8

MIX_TILE = 512
POOL_HALO = 16
CONV_HALO = 32
CONV_ROWS = 64
SORT_TILE = 256
EXPERT_BLOCK = 512
VMEM_LIMIT = 56 * 1024 * 1024

_F32 = jnp.float32
_BF16 = jnp.bfloat16


def _rms(x, g):
    return x * lax.rsqrt(jnp.mean(x * x, axis=-1, keepdims=True) + RMS_EPS) * g


def _sigmoid(x):
    return 1.0 / (1.0 + jnp.exp(-x))


def _dot(a, b):
    return jnp.dot(a, b, preferred_element_type=_F32)


def _fold_groups(v):
    n, d = v.shape
    left = v[:, :d // 2].reshape(n // SUBLANES, SUBLANES, d // 2)
    right = v[:, d // 2:].reshape(n // SUBLANES, SUBLANES, d // 2)
    return jnp.concatenate([left, right], axis=1).astype(_BF16)


def _unfold_groups(w):
    g, _, half = w.shape
    f = w.astype(_F32)
    left = f[:, :SUBLANES, :].reshape(g * SUBLANES, half)
    right = f[:, SUBLANES:, :].reshape(g * SUBLANES, half)
    return jnp.concatenate([left, right], axis=-1).astype(_BF16)


def _mixer_kernel(x_ref, meta_ref, gmix_ref, win_ref, bin_ref, wpool_ref, pscale_ref,
                  kdw_ref, bdw_ref, lng_ref, lnb_ref, wpw2_ref, bpw2_ref, wout_ref,
                  gffn_ref, wr_ref, rb_ref, tri_ref, lmat_ref,
                  h1_ref, hn_ref, gate_ref, slot_ref, segend_ref, off_ref, tot_ref,
                  ubuf, abuf, base):
    b = pl.program_id(0)
    j = pl.program_id(1)
    tq = x_ref.shape[1]
    c0 = POOL_WIDTH
    c1 = c0 + CONV_WIDTH
    c2 = c1 + CONV_WIDTH
    d_model = x_ref.shape[2]
    c3 = c2 + d_model
    n_slab = CONV_WIDTH // LANES

    @pl.when((b == 0) & (j == 0))
    def _():
        base[...] = jnp.zeros_like(base)

    @pl.when(j == 0)
    def _():
        xm = _rms(meta_ref[...], gmix_ref[...]).astype(_BF16)
        pm = _dot(xm, win_ref[:, 0:c2]) + bin_ref[:, 0:c2]
        um = pm[:, 0:c0]
        am = pm[:, c0:c1] * _sigmoid(pm[:, c1:c2])
        for s in range(n_slab):
            ubuf[s, 0:POOL_HALO, :] = um[:, s * LANES:(s + 1) * LANES]
            abuf[s, 0:CONV_HALO - N_META, :] = jnp.zeros((CONV_HALO - N_META, LANES), _F32)
            abuf[s, CONV_HALO - N_META:CONV_HALO, :] = am[:, s * LANES:(s + 1) * LANES]

    ts = tri_ref.shape[0]
    n_sub = tq // ts
    nt_dims = (((1,), (1,)), ((), ()))
    eidx = lax.broadcasted_iota(jnp.int32, (N_EXPERTS, ts), 0).astype(_F32)

    def proj_stage(h):
        r0 = h * ts
        xn = _rms(x_ref[0, r0:r0 + ts, :], gmix_ref[...]).astype(_BF16)
        u = _dot(xn, win_ref[:, 0:c0]) + bin_ref[:, 0:c0]
        v = _dot(xn, win_ref[:, c0:c1]) + bin_ref[:, c0:c1]
        gt = _dot(xn, win_ref[:, c1:c2]) + bin_ref[:, c1:c2]
        a = v * _sigmoid(gt)
        for s in range(n_slab):
            ubuf[s, POOL_HALO + r0:POOL_HALO + r0 + ts, :] = u[:, s * LANES:(s + 1) * LANES]
            abuf[s, CONV_HALO + r0:CONV_HALO + r0 + ts, :] = a[:, s * LANES:(s + 1) * LANES]
        return xn

    def conv_stage(h):
        r0 = h * ts
        conv = []
        for s in range(n_slab):
            lo = s * LANES
            rows = []
            for rc in range(r0, r0 + ts, CONV_ROWS):
                acc = jnp.broadcast_to(bdw_ref[:, lo:lo + LANES], (CONV_ROWS, LANES))
                for tap in range(CONV_KERNEL):
                    start = CONV_HALO - (CONV_KERNEL - 1) + tap + rc
                    acc = acc + kdw_ref[tap:tap + 1, lo:lo + LANES] * abuf[s, start:start + CONV_ROWS, :]
                rows.append(acc)
            conv.append(jnp.concatenate(rows, axis=0))
        return jnp.concatenate(conv, axis=-1)

    def mix_stage(h, c, xn):
        r0 = h * ts
        g_a = _dot(xn, win_ref[:, c2:c3]) + bin_ref[:, c2:c3]
        g_b = _dot(xn, win_ref[:, c3:]) + bin_ref[:, c3:]
        ya = []
        for g, w in enumerate(POOL_WINDOWS):
            ug = ubuf[g, POOL_HALO + r0:POOL_HALO + r0 + ts, :]
            acc = ug
            for back in range(1, w):
                acc = acc + ubuf[g, POOL_HALO + r0 - back:POOL_HALO + r0 - back + ts, :]
            dg = acc * (1.0 / w) - ug
            ya.append(_dot(dg.astype(_BF16), wpool_ref[g]))
        y_a = jnp.concatenate(ya, axis=-1) * pscale_ref[...]

        mu = jnp.mean(c, axis=-1, keepdims=True)
        cc = c - mu
        var = jnp.mean(cc * cc, axis=-1, keepdims=True)
        cn = cc * lax.rsqrt(var + LN_EPS) * lng_ref[...] + lnb_ref[...]
        sw = cn * _sigmoid(cn)
        y_b = _dot(sw.astype(_BF16), wpw2_ref[...]) + bpw2_ref[...]

        mixed = _sigmoid(g_a) * y_a + _sigmoid(g_b) * y_b
        h1 = x_ref[0, r0:r0 + ts, :] + _dot(mixed.astype(_BF16), wout_ref[...])
        h1_ref[r0:r0 + ts, :] = h1
        hn_ref[r0:r0 + ts, :] = _rms(h1, gffn_ref[...])

    def route_stage(h):
        r0 = h * ts
        hn = hn_ref[r0:r0 + ts, :]
        hn_hi = hn.astype(_BF16)
        hn_lo = (hn - hn_hi.astype(_F32)).astype(_BF16)
        l_hi = lax.dot_general(wr_ref[...], hn_hi, nt_dims, preferred_element_type=_F32)
        l_lo = lax.dot_general(wr_ref[0:N_EXPERTS, :], hn_lo, nt_dims, preferred_element_type=_F32)
        work = l_hi[0:N_EXPERTS] + l_hi[N_EXPERTS:2 * N_EXPERTS] + l_lo + rb_ref[...]

        vals, hots = [], []
        for _ in range(TOP_K):
            m = jnp.max(work, axis=0, keepdims=True)
            first = jnp.min(jnp.where(work == m, eidx, float(N_EXPERTS)), axis=0, keepdims=True)
            hot = eidx == first
            vals.append(m)
            hots.append(hot)
            work = jnp.where(hot, -jnp.inf, work)
        exps = [jnp.exp(vk - vals[0]) for vk in vals]
        denom = exps[0] + exps[1] + exps[2] + exps[3]
        gate_ref[0, :, r0:r0 + ts] = jnp.concatenate([ek / denom for ek in exps], axis=0)

        chosen = jnp.where(hots[0] | hots[1] | hots[2] | hots[3], 1.0, 0.0)
        groups = jnp.floor((jnp.sum(chosen, axis=1, keepdims=True) + (SUBLANES - 1)) * (1.0 / SUBLANES))
        groups = jnp.broadcast_to(groups, (N_EXPERTS, LANES))
        seg = _dot(lmat_ref[...], groups.astype(_BF16)) * float(SUBLANES)
        place = seg[:, 0:1] + _dot(chosen.astype(_BF16), tri_ref[...])
        slots = [jnp.sum(jnp.where(m, place, 0.0), axis=0, keepdims=True) for m in hots]
        slot_ref[h] = jnp.concatenate(slots, axis=0).astype(jnp.int32)
        segend_ref[h] = seg + groups * float(SUBLANES)
        off_ref[h] = base[...] - seg
        base[...] = base[...] + groups * float(SUBLANES)

    xns = [proj_stage(h) for h in range(n_sub)]
    for h in range(n_sub):
        mix_stage(h, conv_stage(h), xns[h])
        if h > 0:
            route_stage(h - 1)
    route_stage(n_sub - 1)
    tot_ref[...] = base[...]

    for s in range(n_slab):
        ubuf[s, 0:POOL_HALO, :] = ubuf[s, tq:tq + POOL_HALO, :]
        abuf[s, 0:CONV_HALO, :] = abuf[s, tq:tq + CONV_HALO, :]


def _mixer_call(x, meta, gmix, win, bin_, wpool, pscale, kdw, bdw, lng, lnb, wpw2, bpw2,
                wout, gffn, wr, rb, tri, lmat, tq):
    bsz, seq, d = x.shape
    nj = seq // tq
    n_tok = bsz * seq
    n_tiles = bsz * nj
    ts = tri.shape[0]
    n_sub = tq // ts

    def full(arr):
        nd = arr.ndim
        return pl.BlockSpec(arr.shape, lambda b, j, _n=nd: (0,) * _n)

    def tile3(k, w):
        return pl.BlockSpec((k, TOP_K, w), lambda b, j: (b * nj + j, 0, 0))

    per_expert = pl.BlockSpec((n_sub, N_EXPERTS, LANES), lambda b, j: (b * nj + j, 0, 0))
    rows = pl.BlockSpec((tq, d), lambda b, j: (b * nj + j, 0))
    consts = (meta, gmix, win, bin_, wpool, pscale, kdw, bdw, lng, lnb, wpw2, bpw2, wout,
              gffn, wr, rb, tri, lmat)
    return pl.pallas_call(
        _mixer_kernel,
        grid=(bsz, nj),
        in_specs=[pl.BlockSpec((1, tq, d), lambda b, j: (b, j, 0))] + [full(c) for c in consts],
        out_specs=[rows, rows, tile3(1, tq), tile3(n_sub, ts), per_expert, per_expert,
                   pl.BlockSpec((N_EXPERTS, LANES), lambda b, j: (0, 0))],
        out_shape=[jax.ShapeDtypeStruct((n_tok, d), _F32),
                   jax.ShapeDtypeStruct((n_tok, d), _F32),
                   jax.ShapeDtypeStruct((n_tiles, TOP_K, tq), _F32),
                   jax.ShapeDtypeStruct((n_tiles * n_sub, TOP_K, ts), jnp.int32),
                   jax.ShapeDtypeStruct((n_tiles * n_sub, N_EXPERTS, LANES), _F32),
                   jax.ShapeDtypeStruct((n_tiles * n_sub, N_EXPERTS, LANES), _F32),
                   jax.ShapeDtypeStruct((N_EXPERTS, LANES), _F32)],
        scratch_shapes=[pltpu.VMEM((CONV_WIDTH // LANES, POOL_HALO + tq, LANES), _F32),
                        pltpu.VMEM((CONV_WIDTH // LANES, CONV_HALO + tq, LANES), _F32),
                        pltpu.VMEM((N_EXPERTS, LANES), _F32)],
        compiler_params=pltpu.CompilerParams(
            dimension_semantics=("arbitrary", "arbitrary"), vmem_limit_bytes=VMEM_LIMIT),
        name="mixer",
    )(x, *consts)


def _dispatch_kernel(fill_ref, dst_ref, slot_ref, hn_ref, xs_ref, sbuf, zblk, sem, zsem, *, bm):
    i = pl.program_id(0)
    n_sub, _, ts = slot_ref.shape
    nq = sbuf.shape[1]
    ns = nq * SUBLANES
    bg = bm // SUBLANES
    n_blocks = xs_ref.shape[0] // bg

    def zero_fill(lo, hi, size):
        def start(q, carry):
            pltpu.make_async_copy(zblk.at[pl.ds(0, size)], xs_ref.at[pl.ds(q * size, size)], zsem).start()
            return carry

        def drain(q, carry):
            pltpu.make_async_copy(zblk.at[pl.ds(0, size)], xs_ref.at[pl.ds(q * size, size)], zsem).wait()
            return carry

        lax.fori_loop(lo, hi, start, 0)
        lax.fori_loop(lo, hi, drain, 0)

    @pl.when(i == 0)
    def _():
        zblk[...] = jnp.zeros_like(zblk)
        for e in range(N_EXPERTS):
            zero_fill(fill_ref[0, e], fill_ref[1, e], 1)
        zero_fill(fill_ref[2, 0], n_blocks, bg)

    for h in range(n_sub):
        @pl.when(i > 0)
        def _():
            pltpu.make_async_copy(sbuf.at[h], xs_ref.at[pl.ds(0, nq)], sem.at[h]).wait()

        slot = slot_ref[h]
        rows = lax.broadcasted_iota(jnp.int32, (ns, ts), 0)
        sel = jnp.where(rows == slot[0:1, :], 1.0, 0.0)
        for k in range(1, TOP_K):
            sel = sel + jnp.where(rows == slot[k:k + 1, :], 1.0, 0.0)
        sbuf[h] = _fold_groups(_dot(sel.astype(_BF16), hn_ref[h * ts:(h + 1) * ts, :].astype(_BF16)))

        def issue(q, carry):
            pltpu.make_async_copy(sbuf.at[h, q], xs_ref.at[dst_ref[0, h, q]], sem.at[h]).start()
            return carry

        for q in range(nq):
            issue(q, 0)

    @pl.when(i == pl.num_programs(0) - 1)
    def _():
        for h in range(n_sub):
            pltpu.make_async_copy(sbuf.at[h], xs_ref.at[pl.ds(0, nq)], sem.at[h]).wait()


def _dispatch_call(fill, dst, slot, hn, n_groups, tq, ns, bm):
    n_tok, d = hn.shape
    n_sub, ts = dst.shape[1], slot.shape[2]
    return pl.pallas_call(
        functools.partial(_dispatch_kernel, bm=bm),
        grid_spec=pltpu.PrefetchScalarGridSpec(
            num_scalar_prefetch=1,
            grid=(n_tok // tq,),
            in_specs=[pl.BlockSpec((1, n_sub, ns // SUBLANES), lambda i, f: (i, 0, 0),
                                   memory_space=pltpu.SMEM),
                      pl.BlockSpec((n_sub, TOP_K, ts), lambda i, f: (i, 0, 0)),
                      pl.BlockSpec((tq, d), lambda i, f: (i, 0))],
            out_specs=pl.BlockSpec(memory_space=pl.ANY),
            scratch_shapes=[pltpu.VMEM((n_sub, ns // SUBLANES, 2 * SUBLANES, d // 2), _BF16),
                            pltpu.VMEM((bm // SUBLANES, 2 * SUBLANES, d // 2), _BF16),
                            pltpu.SemaphoreType.DMA((n_sub,)), pltpu.SemaphoreType.DMA]),
        out_shape=jax.ShapeDtypeStruct((n_groups, 2 * SUBLANES, d // 2), _BF16),
        compiler_params=pltpu.CompilerParams(
            dimension_semantics=("arbitrary",), vmem_limit_bytes=VMEM_LIMIT),
        name="dispatch",
    )(fill, dst, slot, hn)


def _experts_kernel(nb_ref, r0_ref, nv_ref, xs_ref, wgu_ref, bgu_ref, wdn_ref, bdn_ref, ys_ref,
                    wgu_bf, wdn_bf, xbuf, ybuf, xsem, ysem, *, n_blocks):
    e = pl.program_id(0)
    n_exp = pl.num_programs(0)
    bg = xbuf.shape[1]
    d_exp = wdn_ref.shape[1]
    nb = nb_ref[e]

    def x_copy(ex, j, slot):
        at = r0_ref[ex] + j * bg
        return pltpu.make_async_copy(xs_ref.at[pl.ds(at, bg)], xbuf.at[slot], xsem.at[slot])

    def y_copy(j, slot):
        at = r0_ref[e] + j * bg
        return pltpu.make_async_copy(ybuf.at[slot], ys_ref.at[pl.ds(at, bg)], ysem.at[slot])

    @pl.when((e == 0) & (nb > 0))
    def _():
        x_copy(e, 0, 0).start()

    @pl.when(nb > 0)
    def _():
        wgu_bf[...] = wgu_ref[0].astype(_BF16)
        wdn_bf[...] = wdn_ref[0].astype(_BF16)

    def block(j, carry):
        slot = lax.rem(j, 2)
        x_copy(e, j, slot).wait()

        @pl.when(j + 1 < nb)
        def _():
            x_copy(e, j + 1, 1 - slot).start()

        @pl.when(j >= 2)
        def _():
            y_copy(j - 2, slot).wait()

        xb = _unfold_groups(xbuf[slot])
        gu = _dot(xb, wgu_bf[...]) + bgu_ref[0]
        gate = jnp.minimum(gu[:, 0:d_exp], SWIGLU_LIMIT)
        up = jnp.clip(gu[:, d_exp:], -SWIGLU_LIMIT, SWIGLU_LIMIT)
        act = (up + 1.0) * (gate * _sigmoid(gate * SWIGLU_ALPHA))
        ybuf[slot] = _fold_groups(_dot(act.astype(_BF16), wdn_bf[...]) + bdn_ref[0])
        y_copy(j, slot).start()
        return carry

    lax.fori_loop(0, nb, block, 0)

    nxt = jnp.minimum(e + 1, n_exp - 1)

    @pl.when((e + 1 < n_exp) & (nb_ref[nxt] > 0))
    def _():
        x_copy(nxt, 0, 0).start()

    @pl.when(nb >= 2)
    def _():
        y_copy(nb - 2, lax.rem(nb, 2)).wait()

    @pl.when(nb >= 1)
    def _():
        y_copy(nb - 1, lax.rem(nb + 1, 2)).wait()

    @pl.when(e == n_exp - 1)
    def _():
        ybuf[0] = jnp.zeros(ybuf.shape[1:], _BF16)

        def start(q, carry):
            pltpu.make_async_copy(ybuf.at[0], ys_ref.at[pl.ds(q * bg, bg)], ysem.at[0]).start()
            return carry

        def drain(q, carry):
            pltpu.make_async_copy(ybuf.at[0], ys_ref.at[pl.ds(q * bg, bg)], ysem.at[0]).wait()
            return carry

        lax.fori_loop(nv_ref[0], n_blocks, start, 0)
        lax.fori_loop(nv_ref[0], n_blocks, drain, 0)


def _experts_call(nb, r0, n_valid, xs, wgu, bgu, wdn, bdn, bm, n_blocks):
    n_exp, d, d_gu = wgu.shape
    d_exp = wdn.shape[1]
    return pl.pallas_call(
        functools.partial(_experts_kernel, n_blocks=n_blocks),
        grid_spec=pltpu.PrefetchScalarGridSpec(
            num_scalar_prefetch=3,
            grid=(n_exp,),
            in_specs=[pl.BlockSpec(memory_space=pl.ANY),
                      pl.BlockSpec((1, d, d_gu), lambda e, *_: (e, 0, 0)),
                      pl.BlockSpec((1, 1, d_gu), lambda e, *_: (e, 0, 0)),
                      pl.BlockSpec((1, d_exp, d), lambda e, *_: (e, 0, 0)),
                      pl.BlockSpec((1, 1, d), lambda e, *_: (e, 0, 0))],
            out_specs=pl.BlockSpec(memory_space=pl.ANY),
            scratch_shapes=[pltpu.VMEM((d, d_gu), _BF16), pltpu.VMEM((d_exp, d), _BF16),
                            pltpu.VMEM((2, bm // SUBLANES, 2 * SUBLANES, d // 2), _BF16),
                            pltpu.VMEM((2, bm // SUBLANES, 2 * SUBLANES, d // 2), _BF16),
                            pltpu.SemaphoreType.DMA((2,)), pltpu.SemaphoreType.DMA((2,))]),
        out_shape=jax.ShapeDtypeStruct((n_blocks * bm // SUBLANES, 2 * SUBLANES, d // 2), _BF16),
        compiler_params=pltpu.CompilerParams(
            dimension_semantics=("arbitrary",), vmem_limit_bytes=VMEM_LIMIT),
        name="experts",
    )(nb, r0, n_valid, xs, wgu, bgu, wdn, bdn)


def _combine_kernel(src_ref, nxt_ref, slot_ref, gate_ref, h1_ref, gfin_ref, ys_ref, out_ref,
                    ybuf, sem):
    i = pl.program_id(0)
    n_sub, nq = ybuf.shape[:2]
    ns = nq * SUBLANES
    ts = h1_ref.shape[0] // n_sub

    def gather(tab_ref, h_from, h_to):
        def issue(q, carry):
            pltpu.make_async_copy(ys_ref.at[tab_ref[0, h_from, q]], ybuf.at[h_to, q],
                                  sem.at[h_to]).start()
            return carry

        for q in range(nq):
            issue(q, 0)

    @pl.when(i == 0)
    def _():
        gather(src_ref, 0, 0)

    for h in range(n_sub):
        if h + 1 < n_sub:
            gather(src_ref, h + 1, h + 1)
        pltpu.make_async_copy(ys_ref.at[pl.ds(0, nq)], ybuf.at[h], sem.at[h]).wait()

        slot = slot_ref[h * ts:(h + 1) * ts, :]
        gate = gate_ref[h * ts:(h + 1) * ts, :]
        cols = lax.broadcasted_iota(jnp.int32, (ts, ns), 1)
        g = jnp.where(cols == slot[:, 0:1], gate[:, 0:1], 0.0)
        for k in range(1, TOP_K):
            g = g + jnp.where(cols == slot[:, k:k + 1], gate[:, k:k + 1], 0.0)
        g_hi = g.astype(_BF16)
        g_lo = (g - g_hi.astype(_F32)).astype(_BF16)
        yb = _unfold_groups(ybuf[h])
        if h + 1 == n_sub:
            @pl.when(i < pl.num_programs(0) - 1)
            def _():
                gather(nxt_ref, 0, 0)
        acc = h1_ref[h * ts:(h + 1) * ts, :] + _dot(g_hi, yb) + _dot(g_lo, yb)
        out_ref[h * ts:(h + 1) * ts, :] = _rms(acc, gfin_ref[...])


def _combine_call(src, slot_t, gate_t, h1, gfin, ys, tq, ns):
    n_tok, d = h1.shape
    n_steps, n_sub, nq = src.shape
    table = lambda f: pl.BlockSpec((1, n_sub, nq), f, memory_space=pltpu.SMEM)
    return pl.pallas_call(
        _combine_kernel,
        grid=(n_steps,),
        in_specs=[table(lambda i: (i, 0, 0)),
                  table(lambda i: (jnp.minimum(i + 1, n_steps - 1), 0, 0)),
                  pl.BlockSpec((tq, TOP_K), lambda i: (i, 0)),
                  pl.BlockSpec((tq, TOP_K), lambda i: (i, 0)),
                  pl.BlockSpec((tq, d), lambda i: (i, 0)),
                  pl.BlockSpec((1, d), lambda i: (0, 0)),
                  pl.BlockSpec(memory_space=pl.ANY)],
        out_specs=pl.BlockSpec((tq, d), lambda i: (i, 0)),
        out_shape=jax.ShapeDtypeStruct((n_tok, d), _F32),
        scratch_shapes=[pltpu.VMEM((n_sub, ns // SUBLANES, 2 * SUBLANES, d // 2), _BF16),
                        pltpu.SemaphoreType.DMA((n_sub,))],
        compiler_params=pltpu.CompilerParams(
            dimension_semantics=("arbitrary",), vmem_limit_bytes=VMEM_LIMIT),
        name="combine",
    )(src, src, slot_t, gate_t, h1, gfin, ys)


def kernel(x, meta_tokens, norm_mix_g, w_in, b_in, w_pool_grp, pool_scale, w_dwconv, b_dwconv, conv_ln_g, conv_ln_b, w_pw2, b_pw2, w_out, norm_ffn_g, router_w, router_b, w_gate_up, b_gate_up, w_down, b_down, norm_final_g):
    bsz, seq, d = x.shape
    assert w_in.shape[0] == 1, "one layer"
    tq = min(MIX_TILE, seq)
    ts = min(SORT_TILE, tq)
    n_sub = tq // ts
    bm = EXPERT_BLOCK
    n_tok = bsz * seq
    n_sort = n_tok // ts
    ns = -(-(ts * TOP_K + N_EXPERTS * (SUBLANES - 1)) // LANES) * LANES
    nq = ns // SUBLANES
    n_main = (n_tok * TOP_K + n_sort * N_EXPERTS * (SUBLANES - 1)) // bm + N_EXPERTS
    n_spill = -(-(n_sub * ns) // bm)
    i32 = jnp.int32

    row = lambda a: a.reshape(1, -1)
    wr_t = router_w[0].T
    wr_hi = wr_t.astype(_BF16)
    wr_lo = (wr_t - wr_hi.astype(_F32)).astype(_BF16)
    tri = (jnp.arange(ts)[:, None] < jnp.arange(ts)[None, :]).astype(_BF16)
    lmat = (jnp.arange(N_EXPERTS)[None, :] < jnp.arange(N_EXPERTS)[:, None]).astype(_BF16)

    h1, hn, gate, slot, segend, off, tot = _mixer_call(
        x, meta_tokens, row(norm_mix_g[0]), w_in[0].astype(_BF16), row(b_in[0]),
        w_pool_grp[0].astype(_BF16), row(pool_scale[0]), w_dwconv[0], row(b_dwconv[0]),
        row(conv_ln_g[0]), row(conv_ln_b[0]), w_pw2[0].astype(_BF16), row(b_pw2[0]),
        w_out[0].astype(_BF16), row(norm_ffn_g[0]), jnp.concatenate([wr_hi, wr_lo], axis=0),
        router_b[0].reshape(N_EXPERTS, 1), tri, lmat, tq)

    segend = segend[:, :, 0].astype(i32)
    off = off[:, :, 0].astype(i32)
    total = tot[:, 0].astype(i32)
    padded = (total + bm - 1) // bm * bm
    pend = jnp.cumsum(padded)
    pstart = pend - padded
    n_valid = pend[-1] // bm
    fill = jnp.stack([(pstart + total) // SUBLANES, pend // SUBLANES,
                      jnp.full_like(pend, n_valid)]).astype(i32)

    q_row = jnp.arange(nq, dtype=i32) * SUBLANES
    e_q = jnp.sum((segend[:, None, :] <= q_row[None, :, None]).astype(i32), axis=-1)
    used = q_row[None, :] < segend[:, -1:]
    hot = jnp.minimum(e_q, N_EXPERTS - 1)[..., None] == jnp.arange(N_EXPERTS, dtype=i32)
    to_row = jnp.sum(jnp.where(hot, (pstart[None, :] + off)[:, None, :], 0), axis=-1) + q_row
    spill = n_main * bm + (jnp.arange(n_sort, dtype=i32) % n_sub)[:, None] * ns + q_row
    dst = (jnp.where(used, to_row, spill) // SUBLANES).reshape(n_sort // n_sub, n_sub, nq)
    src = (jnp.where(used, to_row, 0) // SUBLANES).reshape(n_sort // n_sub, n_sub, nq)

    slot_t = slot.transpose(0, 2, 1).reshape(n_tok, TOP_K)
    gate_t = gate.transpose(0, 2, 1).reshape(n_tok, TOP_K)

    xs = _dispatch_call(fill, dst, slot, hn, (n_main + n_spill) * bm // SUBLANES, tq, ns, bm)
    ys = _experts_call((padded // bm).astype(i32), (pstart // SUBLANES).astype(i32),
                       n_valid.reshape(1).astype(i32), xs,
                       w_gate_up[0], b_gate_up[0].reshape(N_EXPERTS, 1, -1),
                       w_down[0], b_down[0].reshape(N_EXPERTS, 1, -1), bm, n_main)
    out = _combine_call(src, slot_t, gate_t, h1, row(norm_final_g), ys, tq, ns)
    return out.reshape(bsz, seq, d)
```

```python
import functools

import jax
import jax.numpy as jnp
from jax import lax
from jax.experimental import pallas as pl
from jax.experimental.pallas import tpu as pltpu

N_META = 16
POOL_WINDOWS = (2, 4, 8, 16)
POOL_GROUP_DIM = 128
POOL_OUT_DIM = 256
POOL_WIDTH = POOL_GROUP_DIM * len(POOL_WINDOWS)
CONV_WIDTH = 512
CONV_KERNEL = 31
N_EXPERTS = 32
TOP_K = 4
SWIGLU_LIMIT = 7.0
SWIGLU_ALPHA = 1.702
RMS_EPS = 1e-5
LN_EPS = 1e-5

LANES = 128
SUBLANES = 8

MIX_TILE = 512
POOL_HALO = 16
CONV_HALO = 32
CONV_ROWS = 64
SORT_TILE = 256
EXPERT_BLOCK = 512
VMEM_LIMIT = 56 * 1024 * 1024

_F32 = jnp.float32
_BF16 = jnp.bfloat16


def _rms(x, g):
    return x * lax.rsqrt(jnp.mean(x * x, axis=-1, keepdims=True) + RMS_EPS) * g


def _sigmoid(x):
    return 1.0 / (1.0 + jnp.exp(-x))


def _dot(a, b):
    return jnp.dot(a, b, preferred_element_type=_F32)


def _fold_groups(v):
    n, d = v.shape
    left = v[:, :d // 2].reshape(n // SUBLANES, SUBLANES, d // 2)
    right = v[:, d // 2:].reshape(n // SUBLANES, SUBLANES, d // 2)
    return jnp.concatenate([left, right], axis=1).astype(_BF16)


def _unfold_groups(w):
    g, _, half = w.shape
    f = w.astype(_F32)
    left = f[:, :SUBLANES, :].reshape(g * SUBLANES, half)
    right = f[:, SUBLANES:, :].reshape(g * SUBLANES, half)
    return jnp.concatenate([left, right], axis=-1).astype(_BF16)


def _mixer_kernel(x_ref, meta_ref, gmix_ref, win_ref, bin_ref, wpool_ref, pscale_ref,
                  kdw_ref, bdw_ref, lng_ref, lnb_ref, wpw2_ref, bpw2_ref, wout_ref,
                  gffn_ref, wr_ref, rb_ref, tri_ref, lmat_ref,
                  h1_ref, hn_ref, gate_ref, slot_ref, segend_ref, off_ref, tot_ref,
                  ubuf, abuf, base):
    b = pl.program_id(0)
    j = pl.program_id(1)
    tq = x_ref.shape[1]
    c0 = POOL_WIDTH
    c1 = c0 + CONV_WIDTH
    c2 = c1 + CONV_WIDTH
    d_model = x_ref.shape[2]
    c3 = c2 + d_model
    n_slab = CONV_WIDTH // LANES

    @pl.when((b == 0) & (j == 0))
    def _():
        base[...] = jnp.zeros_like(base)

    @pl.when(j == 0)
    def _():
        xm = _rms(meta_ref[...], gmix_ref[...]).astype(_BF16)
        pm = _dot(xm, win_ref[:, 0:c2]) + bin_ref[:, 0:c2]
        um = pm[:, 0:c0]
        am = pm[:, c0:c1] * _sigmoid(pm[:, c1:c2])
        for s in range(n_slab):
            ubuf[s, 0:POOL_HALO, :] = um[:, s * LANES:(s + 1) * LANES]
            abuf[s, 0:CONV_HALO - N_META, :] = jnp.zeros((CONV_HALO - N_META, LANES), _F32)
            abuf[s, CONV_HALO - N_META:CONV_HALO, :] = am[:, s * LANES:(s + 1) * LANES]

    ts = tri_ref.shape[0]
    n_sub = tq // ts
    nt_dims = (((1,), (1,)), ((), ()))
    eidx = lax.broadcasted_iota(jnp.int32, (N_EXPERTS, ts), 0).astype(_F32)

    def proj_stage(h):
        r0 = h * ts
        xn = _rms(x_ref[0, r0:r0 + ts, :], gmix_ref[...]).astype(_BF16)
        u = _dot(xn, win_ref[:, 0:c0]) + bin_ref[:, 0:c0]
        v = _dot(xn, win_ref[:, c0:c1]) + bin_ref[:, c0:c1]
        gt = _dot(xn, win_ref[:, c1:c2]) + bin_ref[:, c1:c2]
        a = v * _sigmoid(gt)
        for s in range(n_slab):
            ubuf[s, POOL_HALO + r0:POOL_HALO + r0 + ts, :] = u[:, s * LANES:(s + 1) * LANES]
            abuf[s, CONV_HALO + r0:CONV_HALO + r0 + ts, :] = a[:, s * LANES:(s + 1) * LANES]
        return xn

    def conv_stage(h):
        r0 = h * ts
        conv = []
        for s in range(n_slab):
            lo = s * LANES
            rows = []
            for rc in range(r0, r0 + ts, CONV_ROWS):
                acc = jnp.broadcast_to(bdw_ref[:, lo:lo + LANES], (CONV_ROWS, LANES))
                for tap in range(CONV_KERNEL):
                    start = CONV_HALO - (CONV_KERNEL - 1) + tap + rc
                    acc = acc + kdw_ref[tap:tap + 1, lo:lo + LANES] * abuf[s, start:start + CONV_ROWS, :]
                rows.append(acc)
            conv.append(jnp.concatenate(rows, axis=0))
        return jnp.concatenate(conv, axis=-1)

    def mix_stage(h, c, xn):
        r0 = h * ts
        g_a = _dot(xn, win_ref[:, c2:c3]) + bin_ref[:, c2:c3]
        g_b = _dot(xn, win_ref[:, c3:]) + bin_ref[:, c3:]
        ya = []
        for g, w in enumerate(POOL_WINDOWS):
            ug = ubuf[g, POOL_HALO + r0:POOL_HALO + r0 + ts, :]
            acc = ug
            for back in range(1, w):
                acc = acc + ubuf[g, POOL_HALO + r0 - back:POOL_HALO + r0 - back + ts, :]
            dg = acc * (1.0 / w) - ug
            ya.append(_dot(dg.astype(_BF16), wpool_ref[g]))
        y_a = jnp.concatenate(ya, axis=-1) * pscale_ref[...]

        mu = jnp.mean(c, axis=-1, keepdims=True)
        cc = c - mu
        var = jnp.mean(cc * cc, axis=-1, keepdims=True)
        cn = cc * lax.rsqrt(var + LN_EPS) * lng_ref[...] + lnb_ref[...]
        sw = cn * _sigmoid(cn)
        y_b = _dot(sw.astype(_BF16), wpw2_ref[...]) + bpw2_ref[...]

        mixed = _sigmoid(g_a) * y_a + _sigmoid(g_b) * y_b
        h1 = x_ref[0, r0:r0 + ts, :] + _dot(mixed.astype(_BF16), wout_ref[...])
        h1_ref[r0:r0 + ts, :] = h1
        hn_ref[r0:r0 + ts, :] = _rms(h1, gffn_ref[...])

    def route_stage(h):
        r0 = h * ts
        hn = hn_ref[r0:r0 + ts, :]
        hn_hi = hn.astype(_BF16)
        hn_lo = (hn - hn_hi.astype(_F32)).astype(_BF16)
        l_hi = lax.dot_general(wr_ref[...], hn_hi, nt_dims, preferred_element_type=_F32)
        l_lo = lax.dot_general(wr_ref[0:N_EXPERTS, :], hn_lo, nt_dims, preferred_element_type=_F32)
        work = l_hi[0:N_EXPERTS] + l_hi[N_EXPERTS:2 * N_EXPERTS] + l_lo + rb_ref[...]

        vals, hots = [], []
        for _ in range(TOP_K):
            m = jnp.max(work, axis=0, keepdims=True)
            first = jnp.min(jnp.where(work == m, eidx, float(N_EXPERTS)), axis=0, keepdims=True)
            hot = eidx == first
            vals.append(m)
            hots.append(hot)
            work = jnp.where(hot, -jnp.inf, work)
        exps = [jnp.exp(vk - vals[0]) for vk in vals]
        denom = exps[0] + exps[1] + exps[2] + exps[3]
        gate_ref[0, :, r0:r0 + ts] = jnp.concatenate([ek / denom for ek in exps], axis=0)

        chosen = jnp.where(hots[0] | hots[1] | hots[2] | hots[3], 1.0, 0.0)
        groups = jnp.floor((jnp.sum(chosen, axis=1, keepdims=True) + (SUBLANES - 1)) * (1.0 / SUBLANES))
        groups = jnp.broadcast_to(groups, (N_EXPERTS, LANES))
        seg = _dot(lmat_ref[...], groups.astype(_BF16)) * float(SUBLANES)
        place = seg[:, 0:1] + _dot(chosen.astype(_BF16), tri_ref[...])
        slots = [jnp.sum(jnp.where(m, place, 0.0), axis=0, keepdims=True) for m in hots]
        slot_ref[h] = jnp.concatenate(slots, axis=0).astype(jnp.int32)
        segend_ref[h] = seg + groups * float(SUBLANES)
        off_ref[h] = base[...] - seg
        base[...] = base[...] + groups * float(SUBLANES)

    xns = [proj_stage(h) for h in range(n_sub)]
    for h in range(n_sub):
        mix_stage(h, conv_stage(h), xns[h])
        if h > 0:
            route_stage(h - 1)
    route_stage(n_sub - 1)
    tot_ref[...] = base[...]

    for s in range(n_slab):
        ubuf[s, 0:POOL_HALO, :] = ubuf[s, tq:tq + POOL_HALO, :]
        abuf[s, 0:CONV_HALO, :] = abuf[s, tq:tq + CONV_HALO, :]


def _mixer_call(x, meta, gmix, win, bin_, wpool, pscale, kdw, bdw, lng, lnb, wpw2, bpw2,
                wout, gffn, wr, rb, tri, lmat, tq):
    bsz, seq, d = x.shape
    nj = seq // tq
    n_tok = bsz * seq
    n_tiles = bsz * nj
    ts = tri.shape[0]
    n_sub = tq // ts

    def full(arr):
        nd = arr.ndim
        return pl.BlockSpec(arr.shape, lambda b, j, _n=nd: (0,) * _n)

    def tile3(k, w):
        return pl.BlockSpec((k, TOP_K, w), lambda b, j: (b * nj + j, 0, 0))

    per_expert = pl.BlockSpec((n_sub, N_EXPERTS, LANES), lambda b, j: (b * nj + j, 0, 0))
    rows = pl.BlockSpec((tq, d), lambda b, j: (b * nj + j, 0))
    consts = (meta, gmix, win, bin_, wpool, pscale, kdw, bdw, lng, lnb, wpw2, bpw2, wout,
              gffn, wr, rb, tri, lmat)
    return pl.pallas_call(
        _mixer_kernel,
        grid=(bsz, nj),
        in_specs=[pl.BlockSpec((1, tq, d), lambda b, j: (b, j, 0))] + [full(c) for c in consts],
        out_specs=[rows, rows, tile3(1, tq), tile3(n_sub, ts), per_expert, per_expert,
                   pl.BlockSpec((N_EXPERTS, LANES), lambda b, j: (0, 0))],
        out_shape=[jax.ShapeDtypeStruct((n_tok, d), _F32),
                   jax.ShapeDtypeStruct((n_tok, d), _F32),
                   jax.ShapeDtypeStruct((n_tiles, TOP_K, tq), _F32),
                   jax.ShapeDtypeStruct((n_tiles * n_sub, TOP_K, ts), jnp.int32),
                   jax.ShapeDtypeStruct((n_tiles * n_sub, N_EXPERTS, LANES), _F32),
                   jax.ShapeDtypeStruct((n_tiles * n_sub, N_EXPERTS, LANES), _F32),
                   jax.ShapeDtypeStruct((N_EXPERTS, LANES), _F32)],
        scratch_shapes=[pltpu.VMEM((CONV_WIDTH // LANES, POOL_HALO + tq, LANES), _F32),
                        pltpu.VMEM((CONV_WIDTH // LANES, CONV_HALO + tq, LANES), _F32),
                        pltpu.VMEM((N_EXPERTS, LANES), _F32)],
        compiler_params=pltpu.CompilerParams(
            dimension_semantics=("arbitrary", "arbitrary"), vmem_limit_bytes=VMEM_LIMIT),
        name="mixer",
    )(x, *consts)


def _dispatch_kernel(fill_ref, dst_ref, slot_ref, hn_ref, xs_ref, sbuf, zblk, sem, zsem, *, bm):
    i = pl.program_id(0)
    n_sub, _, ts = slot_ref.shape
    nq = sbuf.shape[1]
    ns = nq * SUBLANES
    bg = bm // SUBLANES
    n_blocks = xs_ref.shape[0] // bg

    def zero_fill(lo, hi, size):
        def start(q, carry):
            pltpu.make_async_copy(zblk.at[pl.ds(0, size)], xs_ref.at[pl.ds(q * size, size)], zsem).start()
            return carry

        def drain(q, carry):
            pltpu.make_async_copy(zblk.at[pl.ds(0, size)], xs_ref.at[pl.ds(q * size, size)], zsem).wait()
            return carry

        lax.fori_loop(lo, hi, start, 0)
        lax.fori_loop(lo, hi, drain, 0)

    @pl.when(i == 0)
    def _():
        zblk[...] = jnp.zeros_like(zblk)
        for e in range(N_EXPERTS):
            zero_fill(fill_ref[0, e], fill_ref[1, e], 1)
        zero_fill(fill_ref[2, 0], n_blocks, bg)

    for h in range(n_sub):
        @pl.when(i > 0)
        def _():
            pltpu.make_async_copy(sbuf.at[h], xs_ref.at[pl.ds(0, nq)], sem.at[h]).wait()

        slot = slot_ref[h]
        rows = lax.broadcasted_iota(jnp.int32, (ns, ts), 0)
        sel = jnp.zeros((ns, ts), _F32)
        for k in range(TOP_K):
            sel = jnp.where(rows == slot[k:k + 1, :], 1.0, sel)
        sbuf[h] = _fold_groups(_dot(sel.astype(_BF16), hn_ref[h * ts:(h + 1) * ts, :].astype(_BF16)))

        def issue(q, carry):
            pltpu.make_async_copy(sbuf.at[h, q], xs_ref.at[dst_ref[0, h, q]], sem.at[h]).start()
            return carry

        for q in range(nq):
            issue(q, 0)

    @pl.when(i == pl.num_programs(0) - 1)
    def _():
        for h in range(n_sub):
            pltpu.make_async_copy(sbuf.at[h], xs_ref.at[pl.ds(0, nq)], sem.at[h]).wait()


def _dispatch_call(fill, dst, slot, hn, n_groups, tq, ns, bm):
    n_tok, d = hn.shape
    n_sub, ts = dst.shape[1], slot.shape[2]
    return pl.pallas_call(
        functools.partial(_dispatch_kernel, bm=bm),
        grid_spec=pltpu.PrefetchScalarGridSpec(
            num_scalar_prefetch=1,
            grid=(n_tok // tq,),
            in_specs=[pl.BlockSpec((1, n_sub, ns // SUBLANES), lambda i, f: (i, 0, 0),
                                   memory_space=pltpu.SMEM),
                      pl.BlockSpec((n_sub, TOP_K, ts), lambda i, f: (i, 0, 0)),
                      pl.BlockSpec((tq, d), lambda i, f: (i, 0))],
            out_specs=pl.BlockSpec(memory_space=pl.ANY),
            scratch_shapes=[pltpu.VMEM((n_sub, ns // SUBLANES, 2 * SUBLANES, d // 2), _BF16),
                            pltpu.VMEM((bm // SUBLANES, 2 * SUBLANES, d // 2), _BF16),
                            pltpu.SemaphoreType.DMA((n_sub,)), pltpu.SemaphoreType.DMA]),
        out_shape=jax.ShapeDtypeStruct((n_groups, 2 * SUBLANES, d // 2), _BF16),
        compiler_params=pltpu.CompilerParams(
            dimension_semantics=("arbitrary",), vmem_limit_bytes=VMEM_LIMIT),
        name="dispatch",
    )(fill, dst, slot, hn)


def _experts_kernel(nb_ref, r0_ref, nv_ref, xs_ref, wgu_ref, bgu_ref, wdn_ref, bdn_ref, ys_ref,
                    wgu_bf, wdn_bf, xbuf, ybuf, xsem, ysem, *, n_blocks):
    e = pl.program_id(0)
    n_exp = pl.num_programs(0)
    bg = xbuf.shape[1]
    d_exp = wdn_ref.shape[1]
    nb = nb_ref[e]

    def x_copy(ex, j, slot):
        at = r0_ref[ex] + j * bg
        return pltpu.make_async_copy(xs_ref.at[pl.ds(at, bg)], xbuf.at[slot], xsem.at[slot])

    def y_copy(j, slot):
        at = r0_ref[e] + j * bg
        return pltpu.make_async_copy(ybuf.at[slot], ys_ref.at[pl.ds(at, bg)], ysem.at[slot])

    @pl.when((e == 0) & (nb > 0))
    def _():
        x_copy(e, 0, 0).start()

    @pl.when(nb > 0)
    def _():
        wgu_bf[...] = wgu_ref[0].astype(_BF16)
        wdn_bf[...] = wdn_ref[0].astype(_BF16)

    def block(j, carry):
        slot = lax.rem(j, 2)
        x_copy(e, j, slot).wait()

        @pl.when(j + 1 < nb)
        def _():
            x_copy(e, j + 1, 1 - slot).start()

        @pl.when(j >= 2)
        def _():
            y_copy(j - 2, slot).wait()

        xb = _unfold_groups(xbuf[slot])
        gu = _dot(xb, wgu_bf[...]) + bgu_ref[0]
        gate = jnp.minimum(gu[:, 0:d_exp], SWIGLU_LIMIT)
        up = jnp.clip(gu[:, d_exp:], -SWIGLU_LIMIT, SWIGLU_LIMIT)
        act = (up + 1.0) * (gate * _sigmoid(gate * SWIGLU_ALPHA))
        ybuf[slot] = _fold_groups(_dot(act.astype(_BF16), wdn_bf[...]) + bdn_ref[0])
        y_copy(j, slot).start()
        return carry

    lax.fori_loop(0, nb, block, 0)

    nxt = jnp.minimum(e + 1, n_exp - 1)

    @pl.when((e + 1 < n_exp) & (nb_ref[nxt] > 0))
    def _():
        x_copy(nxt, 0, 0).start()

    @pl.when(nb >= 2)
    def _():
        y_copy(nb - 2, lax.rem(nb, 2)).wait()

    @pl.when(nb >= 1)
    def _():
        y_copy(nb - 1, lax.rem(nb + 1, 2)).wait()

    @pl.when(e == n_exp - 1)
    def _():
        ybuf[0] = jnp.zeros(ybuf.shape[1:], _BF16)

        def start(q, carry):
            pltpu.make_async_copy(ybuf.at[0], ys_ref.at[pl.ds(q * bg, bg)], ysem.at[0]).start()
            return carry

        def drain(q, carry):
            pltpu.make_async_copy(ybuf.at[0], ys_ref.at[pl.ds(q * bg, bg)], ysem.at[0]).wait()
            return carry

        lax.fori_loop(nv_ref[0], n_blocks, start, 0)
        lax.fori_loop(nv_ref[0], n_blocks, drain, 0)


def _experts_call(nb, r0, n_valid, xs, wgu, bgu, wdn, bdn, bm, n_blocks):
    n_exp, d, d_gu = wgu.shape
    d_exp = wdn.shape[1]
    return pl.pallas_call(
        functools.partial(_experts_kernel, n_blocks=n_blocks),
        grid_spec=pltpu.PrefetchScalarGridSpec(
            num_scalar_prefetch=3,
            grid=(n_exp,),
            in_specs=[pl.BlockSpec(memory_space=pl.ANY),
                      pl.BlockSpec((1, d, d_gu), lambda e, *_: (e, 0, 0)),
                      pl.BlockSpec((1, 1, d_gu), lambda e, *_: (e, 0, 0)),
                      pl.BlockSpec((1, d_exp, d), lambda e, *_: (e, 0, 0)),
                      pl.BlockSpec((1, 1, d), lambda e, *_: (e, 0, 0))],
            out_specs=pl.BlockSpec(memory_space=pl.ANY),
            scratch_shapes=[pltpu.VMEM((d, d_gu), _BF16), pltpu.VMEM((d_exp, d), _BF16),
                            pltpu.VMEM((2, bm // SUBLANES, 2 * SUBLANES, d // 2), _BF16),
                            pltpu.VMEM((2, bm // SUBLANES, 2 * SUBLANES, d // 2), _BF16),
                            pltpu.SemaphoreType.DMA((2,)), pltpu.SemaphoreType.DMA((2,))]),
        out_shape=jax.ShapeDtypeStruct((n_blocks * bm // SUBLANES, 2 * SUBLANES, d // 2), _BF16),
        compiler_params=pltpu.CompilerParams(
            dimension_semantics=("arbitrary",), vmem_limit_bytes=VMEM_LIMIT),
        name="experts",
    )(nb, r0, n_valid, xs, wgu, bgu, wdn, bdn)


def _combine_kernel(src_ref, nxt_ref, slot_ref, gate_ref, h1_ref, gfin_ref, ys_ref, out_ref,
                    ybuf, sem):
    i = pl.program_id(0)
    n_sub, nq = ybuf.shape[:2]
    ns = nq * SUBLANES
    ts = h1_ref.shape[0] // n_sub

    def gather(tab_ref, h_from, h_to):
        def issue(q, carry):
            pltpu.make_async_copy(ys_ref.at[tab_ref[0, h_from, q]], ybuf.at[h_to, q],
                                  sem.at[h_to]).start()
            return carry

        for q in range(nq):
            issue(q, 0)

    @pl.when(i == 0)
    def _():
        gather(src_ref, 0, 0)

    for h in range(n_sub):
        if h + 1 < n_sub:
            gather(src_ref, h + 1, h + 1)
        pltpu.make_async_copy(ys_ref.at[pl.ds(0, nq)], ybuf.at[h], sem.at[h]).wait()

        slot = slot_ref[h * ts:(h + 1) * ts, :]
        gate = gate_ref[h * ts:(h + 1) * ts, :]
        cols = lax.broadcasted_iota(jnp.int32, (ts, ns), 1)
        g = jnp.zeros((ts, ns), _F32)
        for k in range(TOP_K):
            g = jnp.where(cols == slot[:, k:k + 1], gate[:, k:k + 1], g)
        yb = _unfold_groups(ybuf[h])
        if h + 1 == n_sub:
            @pl.when(i < pl.num_programs(0) - 1)
            def _():
                gather(nxt_ref, 0, 0)
        acc = h1_ref[h * ts:(h + 1) * ts, :] + _dot(g.astype(_BF16), yb)
        out_ref[h * ts:(h + 1) * ts, :] = _rms(acc, gfin_ref[...])


def _combine_call(src, slot_t, gate_t, h1, gfin, ys, tq, ns):
    n_tok, d = h1.shape
    n_steps, n_sub, nq = src.shape
    table = lambda f: pl.BlockSpec((1, n_sub, nq), f, memory_space=pltpu.SMEM)
    return pl.pallas_call(
        _combine_kernel,
        grid=(n_steps,),
        in_specs=[table(lambda i: (i, 0, 0)),
                  table(lambda i: (jnp.minimum(i + 1, n_steps - 1), 0, 0)),
                  pl.BlockSpec((tq, TOP_K), lambda i: (i, 0)),
                  pl.BlockSpec((tq, TOP_K), lambda i: (i, 0)),
                  pl.BlockSpec((tq, d), lambda i: (i, 0)),
                  pl.BlockSpec((1, d), lambda i: (0, 0)),
                  pl.BlockSpec(memory_space=pl.ANY)],
        out_specs=pl.BlockSpec((tq, d), lambda i: (i, 0)),
        out_shape=jax.ShapeDtypeStruct((n_tok, d), _F32),
        scratch_shapes=[pltpu.VMEM((n_sub, ns // SUBLANES, 2 * SUBLANES, d // 2), _BF16),
                        pltpu.SemaphoreType.DMA((n_sub,))],
        compiler_params=pltpu.CompilerParams(
            dimension_semantics=("arbitrary",), vmem_limit_bytes=VMEM_LIMIT),
        name="combine",
    )(src, src, slot_t, gate_t, h1, gfin, ys)


def kernel(x, meta_tokens, norm_mix_g, w_in, b_in, w_pool_grp, pool_scale, w_dwconv, b_dwconv, conv_ln_g, conv_ln_b, w_pw2, b_pw2, w_out, norm_ffn_g, router_w, router_b, w_gate_up, b_gate_up, w_down, b_down, norm_final_g):
    bsz, seq, d = x.shape
    assert w_in.shape[0] == 1, "one layer"
    tq = min(MIX_TILE, seq)
    ts = min(SORT_TILE, tq)
    n_sub = tq // ts
    bm = EXPERT_BLOCK
    n_tok = bsz * seq
    n_sort = n_tok // ts
    ns = -(-(ts * TOP_K + N_EXPERTS * (SUBLANES - 1)) // LANES) * LANES
    nq = ns // SUBLANES
    n_main = (n_tok * TOP_K + n_sort * N_EXPERTS * (SUBLANES - 1)) // bm + N_EXPERTS
    n_spill = -(-(n_sub * ns) // bm)
    i32 = jnp.int32

    row = lambda a: a.reshape(1, -1)
    wr_t = router_w[0].T
    wr_hi = wr_t.astype(_BF16)
    wr_lo = (wr_t - wr_hi.astype(_F32)).astype(_BF16)
    tri = (jnp.arange(ts)[:, None] < jnp.arange(ts)[None, :]).astype(_BF16)
    lmat = (jnp.arange(N_EXPERTS)[None, :] < jnp.arange(N_EXPERTS)[:, None]).astype(_BF16)

    h1, hn, gate, slot, segend, off, tot = _mixer_call(
        x, meta_tokens, row(norm_mix_g[0]), w_in[0].astype(_BF16), row(b_in[0]),
        w_pool_grp[0].astype(_BF16), row(pool_scale[0]), w_dwconv[0], row(b_dwconv[0]),
        row(conv_ln_g[0]), row(conv_ln_b[0]), w_pw2[0].astype(_BF16), row(b_pw2[0]),
        w_out[0].astype(_BF16), row(norm_ffn_g[0]), jnp.concatenate([wr_hi, wr_lo], axis=0),
        router_b[0].reshape(N_EXPERTS, 1), tri, lmat, tq)

    segend = segend[:, :, 0].astype(i32)
    off = off[:, :, 0].astype(i32)
    total = tot[:, 0].astype(i32)
    padded = (total + bm - 1) // bm * bm
    pend = jnp.cumsum(padded)
    pstart = pend - padded
    n_valid = pend[-1] // bm
    fill = jnp.stack([(pstart + total) // SUBLANES, pend // SUBLANES,
                      jnp.full_like(pend, n_valid)]).astype(i32)

    q_row = jnp.arange(nq, dtype=i32) * SUBLANES
    e_q = jnp.sum((segend[:, None, :] <= q_row[None, :, None]).astype(i32), axis=-1)
    used = q_row[None, :] < segend[:, -1:]
    hot = jnp.minimum(e_q, N_EXPERTS - 1)[..., None] == jnp.arange(N_EXPERTS, dtype=i32)
    to_row = jnp.sum(jnp.where(hot, (pstart[None, :] + off)[:, None, :], 0), axis=-1) + q_row
    spill = n_main * bm + (jnp.arange(n_sort, dtype=i32) % n_sub)[:, None] * ns + q_row
    dst = (jnp.where(used, to_row, spill) // SUBLANES).reshape(n_sort // n_sub, n_sub, nq)
    src = (jnp.where(used, to_row, 0) // SUBLANES).reshape(n_sort // n_sub, n_sub, nq)

    slot_t = slot.transpose(0, 2, 1).reshape(n_tok, TOP_K)
    gate_t = gate.transpose(0, 2, 1).reshape(n_tok, TOP_K)

    xs = _dispatch_call(fill, dst, slot, hn, (n_main + n_spill) * bm // SUBLANES, tq, ns, bm)
    ys = _experts_call((padded // bm).astype(i32), (pstart // SUBLANES).astype(i32),
                       n_valid.reshape(1).astype(i32), xs,
                       w_gate_up[0], b_gate_up[0].reshape(N_EXPERTS, 1, -1),
                       w_down[0], b_down[0].reshape(N_EXPERTS, 1, -1), bm, n_main)
    out = _combine_call(src, slot_t, gate_t, h1, row(norm_final_g), ys, tq, ns)
    return out.reshape(bsz, seq, d)
```

```python
import functools

import jax
import jax.numpy as jnp
from jax import lax
from jax.experimental import pallas as pl
from jax.experimental.pallas import tpu as pltpu

N_META = 16
POOL_WINDOWS = (2, 4, 8, 16)
POOL_GROUP_DIM = 128
POOL_OUT_DIM = 256
POOL_WIDTH = POOL_GROUP_DIM * len(POOL_WINDOWS)
CONV_WIDTH = 512
CONV_KERNEL = 31
N_EXPERTS = 32
TOP_K = 4
SWIGLU_LIMIT = 7.0
SWIGLU_ALPHA = 1.702
RMS_EPS = 1e-5
LN_EPS = 1e-5

LANES = 128
SUBLANES = 8

MIX_TILE = 512
POOL_HALO = 16
CONV_HALO = 32
CONV_ROWS = 64
SORT_TILE = 256
EXPERT_BLOCK = 512
VMEM_LIMIT = 56 * 1024 * 1024

_F32 = jnp.float32
_BF16 = jnp.bfloat16


def _rms(x, g):
    return x * lax.rsqrt(jnp.mean(x * x, axis=-1, keepdims=True) + RMS_EPS) * g


def _sigmoid(x):
    return 1.0 / (1.0 + jnp.exp(-x))


def _dot(a, b):
    return jnp.dot(a, b, preferred_element_type=_F32)


def _fold_groups(v):
    n, d = v.shape
    left = v[:, :d // 2].reshape(n // SUBLANES, SUBLANES, d // 2)
    right = v[:, d // 2:].reshape(n // SUBLANES, SUBLANES, d // 2)
    return jnp.concatenate([left, right], axis=1).astype(_BF16)


def _unfold_groups(w):
    g, _, half = w.shape
    f = w.astype(_F32)
    left = f[:, :SUBLANES, :].reshape(g * SUBLANES, half)
    right = f[:, SUBLANES:, :].reshape(g * SUBLANES, half)
    return jnp.concatenate([left, right], axis=-1).astype(_BF16)


def _mixer_kernel(x_ref, meta_ref, gmix_ref, win_ref, bin_ref, wpool_ref, pscale_ref,
                  kdw_ref, bdw_ref, lng_ref, lnb_ref, wpw2_ref, bpw2_ref, wout_ref,
                  gffn_ref, wr_ref, rb_ref, tri_ref, lmat_ref,
                  h1_ref, hn_ref, gate_ref, slot_ref, segend_ref, off_ref, tot_ref,
                  ubuf, abuf, base, hn32):
    b = pl.program_id(0)
    j = pl.program_id(1)
    tq = x_ref.shape[1]
    c0 = POOL_WIDTH
    c1 = c0 + CONV_WIDTH
    c2 = c1 + CONV_WIDTH
    d_model = x_ref.shape[2]
    c3 = c2 + d_model
    n_slab = CONV_WIDTH // LANES

    @pl.when((b == 0) & (j == 0))
    def _():
        base[...] = jnp.zeros_like(base)

    @pl.when(j == 0)
    def _():
        xm = _rms(meta_ref[...], gmix_ref[...]).astype(_BF16)
        pm = _dot(xm, win_ref[:, 0:c2]) + bin_ref[:, 0:c2]
        um = pm[:, 0:c0]
        am = pm[:, c0:c1] * _sigmoid(pm[:, c1:c2])
        for s in range(n_slab):
            ubuf[s, 0:POOL_HALO, :] = um[:, s * LANES:(s + 1) * LANES]
            abuf[s, 0:CONV_HALO - N_META, :] = jnp.zeros((CONV_HALO - N_META, LANES), _F32)
            abuf[s, CONV_HALO - N_META:CONV_HALO, :] = am[:, s * LANES:(s + 1) * LANES]

    ts = tri_ref.shape[0]
    n_sub = tq // ts
    nt_dims = (((1,), (1,)), ((), ()))
    eidx = lax.broadcasted_iota(jnp.int32, (N_EXPERTS, ts), 0).astype(_F32)

    def proj_stage(h):
        r0 = h * ts
        xn = _rms(x_ref[0, r0:r0 + ts, :], gmix_ref[...]).astype(_BF16)
        u = _dot(xn, win_ref[:, 0:c0]) + bin_ref[:, 0:c0]
        v = _dot(xn, win_ref[:, c0:c1]) + bin_ref[:, c0:c1]
        gt = _dot(xn, win_ref[:, c1:c2]) + bin_ref[:, c1:c2]
        a = v * _sigmoid(gt)
        for s in range(n_slab):
            ubuf[s, POOL_HALO + r0:POOL_HALO + r0 + ts, :] = u[:, s * LANES:(s + 1) * LANES]
            abuf[s, CONV_HALO + r0:CONV_HALO + r0 + ts, :] = a[:, s * LANES:(s + 1) * LANES]
        return xn

    def conv_stage(h):
        r0 = h * ts
        conv = []
        for s in range(n_slab):
            lo = s * LANES
            rows = []
            for rc in range(r0, r0 + ts, CONV_ROWS):
                acc = jnp.broadcast_to(bdw_ref[:, lo:lo + LANES], (CONV_ROWS, LANES))
                for tap in range(CONV_KERNEL):
                    start = CONV_HALO - (CONV_KERNEL - 1) + tap + rc
                    acc = acc + kdw_ref[tap:tap + 1, lo:lo + LANES] * abuf[s, start:start + CONV_ROWS, :]
                rows.append(acc)
            conv.append(jnp.concatenate(rows, axis=0))
        return jnp.concatenate(conv, axis=-1)

    def mix_stage(h, c, xn):
        r0 = h * ts
        g_a = _dot(xn, win_ref[:, c2:c3]) + bin_ref[:, c2:c3]
        g_b = _dot(xn, win_ref[:, c3:]) + bin_ref[:, c3:]
        ya = []
        for g, w in enumerate(POOL_WINDOWS):
            ug = ubuf[g, POOL_HALO + r0:POOL_HALO + r0 + ts, :]
            acc = ug
            for back in range(1, w):
                acc = acc + ubuf[g, POOL_HALO + r0 - back:POOL_HALO + r0 - back + ts, :]
            dg = acc * (1.0 / w) - ug
            ya.append(_dot(dg.astype(_BF16), wpool_ref[g]))
        y_a = jnp.concatenate(ya, axis=-1) * pscale_ref[...]

        mu = jnp.mean(c, axis=-1, keepdims=True)
        cc = c - mu
        var = jnp.mean(cc * cc, axis=-1, keepdims=True)
        cn = cc * lax.rsqrt(var + LN_EPS) * lng_ref[...] + lnb_ref[...]
        sw = cn * _sigmoid(cn)
        y_b = _dot(sw.astype(_BF16), wpw2_ref[...]) + bpw2_ref[...]

        mixed = _sigmoid(g_a) * y_a + _sigmoid(g_b) * y_b
        h1 = x_ref[0, r0:r0 + ts, :] + _dot(mixed.astype(_BF16), wout_ref[...])
        h1_ref[r0:r0 + ts, :] = h1
        hn = _rms(h1, gffn_ref[...])
        hn32[r0:r0 + ts, :] = hn
        hn_ref[r0:r0 + ts, :] = hn.astype(_BF16)

    def route_stage(h):
        r0 = h * ts
        hn = hn32[r0:r0 + ts, :]
        hn_hi = hn.astype(_BF16)
        hn_lo = (hn - hn_hi.astype(_F32)).astype(_BF16)
        l_hi = lax.dot_general(wr_ref[...], hn_hi, nt_dims, preferred_element_type=_F32)
        l_lo = lax.dot_general(wr_ref[0:N_EXPERTS, :], hn_lo, nt_dims, preferred_element_type=_F32)
        work = l_hi[0:N_EXPERTS] + l_hi[N_EXPERTS:2 * N_EXPERTS] + l_lo + rb_ref[...]

        vals, hots = [], []
        for _ in range(TOP_K):
            m = jnp.max(work, axis=0, keepdims=True)
            first = jnp.min(jnp.where(work == m, eidx, float(N_EXPERTS)), axis=0, keepdims=True)
            hot = eidx == first
            vals.append(m)
            hots.append(hot)
            work = jnp.where(hot, -jnp.inf, work)
        exps = [jnp.exp(vk - vals[0]) for vk in vals]
        denom = exps[0] + exps[1] + exps[2] + exps[3]
        gate_ref[0, :, r0:r0 + ts] = jnp.concatenate([ek / denom for ek in exps], axis=0)

        chosen = jnp.where(hots[0] | hots[1] | hots[2] | hots[3], 1.0, 0.0)
        groups = jnp.floor((jnp.sum(chosen, axis=1, keepdims=True) + (SUBLANES - 1)) * (1.0 / SUBLANES))
        groups = jnp.broadcast_to(groups, (N_EXPERTS, LANES))
        seg = _dot(lmat_ref[...], groups.astype(_BF16)) * float(SUBLANES)
        place = seg[:, 0:1] + _dot(chosen.astype(_BF16), tri_ref[...])
        slots = [jnp.sum(jnp.where(m, place, 0.0), axis=0, keepdims=True) for m in hots]
        slot_ref[h] = jnp.concatenate(slots, axis=0).astype(jnp.int32)
        segend_ref[h] = seg + groups * float(SUBLANES)
        off_ref[h] = base[...] - seg
        base[...] = base[...] + groups * float(SUBLANES)

    xns = [proj_stage(h) for h in range(n_sub)]
    for h in range(n_sub):
        mix_stage(h, conv_stage(h), xns[h])
        if h > 0:
            route_stage(h - 1)
    route_stage(n_sub - 1)
    tot_ref[...] = base[...]

    for s in range(n_slab):
        ubuf[s, 0:POOL_HALO, :] = ubuf[s, tq:tq + POOL_HALO, :]
        abuf[s, 0:CONV_HALO, :] = abuf[s, tq:tq + CONV_HALO, :]


def _mixer_call(x, meta, gmix, win, bin_, wpool, pscale, kdw, bdw, lng, lnb, wpw2, bpw2,
                wout, gffn, wr, rb, tri, lmat, tq):
    bsz, seq, d = x.shape
    nj = seq // tq
    n_tok = bsz * seq
    n_tiles = bsz * nj
    ts = tri.shape[0]
    n_sub = tq // ts

    def full(arr):
        nd = arr.ndim
        return pl.BlockSpec(arr.shape, lambda b, j, _n=nd: (0,) * _n)

    def tile3(k, w):
        return pl.BlockSpec((k, TOP_K, w), lambda b, j: (b * nj + j, 0, 0))

    per_expert = pl.BlockSpec((n_sub, N_EXPERTS, LANES), lambda b, j: (b * nj + j, 0, 0))
    rows = pl.BlockSpec((tq, d), lambda b, j: (b * nj + j, 0))
    consts = (meta, gmix, win, bin_, wpool, pscale, kdw, bdw, lng, lnb, wpw2, bpw2, wout,
              gffn, wr, rb, tri, lmat)
    return pl.pallas_call(
        _mixer_kernel,
        grid=(bsz, nj),
        in_specs=[pl.BlockSpec((1, tq, d), lambda b, j: (b, j, 0))] + [full(c) for c in consts],
        out_specs=[rows, rows, tile3(1, tq), tile3(n_sub, ts), per_expert, per_expert,
                   pl.BlockSpec((N_EXPERTS, LANES), lambda b, j: (0, 0))],
        out_shape=[jax.ShapeDtypeStruct((n_tok, d), _F32),
                   jax.ShapeDtypeStruct((n_tok, d), _BF16),
                   jax.ShapeDtypeStruct((n_tiles, TOP_K, tq), _F32),
                   jax.ShapeDtypeStruct((n_tiles * n_sub, TOP_K, ts), jnp.int32),
                   jax.ShapeDtypeStruct((n_tiles * n_sub, N_EXPERTS, LANES), _F32),
                   jax.ShapeDtypeStruct((n_tiles * n_sub, N_EXPERTS, LANES), _F32),
                   jax.ShapeDtypeStruct((N_EXPERTS, LANES), _F32)],
        scratch_shapes=[pltpu.VMEM((CONV_WIDTH // LANES, POOL_HALO + tq, LANES), _F32),
                        pltpu.VMEM((CONV_WIDTH // LANES, CONV_HALO + tq, LANES), _F32),
                        pltpu.VMEM((N_EXPERTS, LANES), _F32),
                        pltpu.VMEM((tq, d), _F32)],
        compiler_params=pltpu.CompilerParams(
            dimension_semantics=("arbitrary", "arbitrary"), vmem_limit_bytes=VMEM_LIMIT),
        name="mixer",
    )(x, *consts)


def _dispatch_kernel(fill_ref, dst_ref, slot_ref, hn_ref, xs_ref, sbuf, zblk, sem, zsem, *, bm):
    i = pl.program_id(0)
    n_sub, _, ts = slot_ref.shape
    nq = sbuf.shape[1]
    ns = nq * SUBLANES
    bg = bm // SUBLANES
    n_blocks = xs_ref.shape[0] // bg

    def zero_fill(lo, hi, size, wait):
        def step(q, carry):
            cp = pltpu.make_async_copy(zblk.at[pl.ds(0, size)], xs_ref.at[pl.ds(q * size, size)], zsem)
            cp.wait() if wait else cp.start()
            return carry

        lax.fori_loop(lo, hi, step, 0)

    @pl.when(i == 0)
    def _():
        zblk[...] = jnp.zeros_like(zblk)
        for wait in (False, True):
            for e in range(N_EXPERTS):
                zero_fill(fill_ref[0, e], fill_ref[1, e], 1, wait)
            zero_fill(fill_ref[2, 0], n_blocks, bg, wait)

    for h in range(n_sub):
        @pl.when(i > 0)
        def _():
            pltpu.make_async_copy(sbuf.at[h], xs_ref.at[pl.ds(0, nq)], sem.at[h]).wait()

        slot = slot_ref[h]
        rows = lax.broadcasted_iota(jnp.int32, (ns, ts), 0)
        sel = jnp.zeros((ns, ts), _F32)
        for k in range(TOP_K):
            sel = jnp.where(rows == slot[k:k + 1, :], 1.0, sel)
        sbuf[h] = _fold_groups(_dot(sel.astype(_BF16), hn_ref[h * ts:(h + 1) * ts, :]))

        def issue(q, carry):
            pltpu.make_async_copy(sbuf.at[h, q], xs_ref.at[dst_ref[0, h, q]], sem.at[h]).start()
            return carry

        for q in range(nq):
            issue(q, 0)

    @pl.when(i == pl.num_programs(0) - 1)
    def _():
        for h in range(n_sub):
            pltpu.make_async_copy(sbuf.at[h], xs_ref.at[pl.ds(0, nq)], sem.at[h]).wait()


def _dispatch_call(fill, dst, slot, hn, n_groups, tq, ns, bm):
    n_tok, d = hn.shape
    n_sub, ts = dst.shape[1], slot.shape[2]
    return pl.pallas_call(
        functools.partial(_dispatch_kernel, bm=bm),
        grid_spec=pltpu.PrefetchScalarGridSpec(
            num_scalar_prefetch=1,
            grid=(n_tok // tq,),
            in_specs=[pl.BlockSpec((1, n_sub, ns // SUBLANES), lambda i, f: (i, 0, 0),
                                   memory_space=pltpu.SMEM),
                      pl.BlockSpec((n_sub, TOP_K, ts), lambda i, f: (i, 0, 0)),
                      pl.BlockSpec((tq, d), lambda i, f: (i, 0))],
            out_specs=pl.BlockSpec(memory_space=pl.ANY),
            scratch_shapes=[pltpu.VMEM((n_sub, ns // SUBLANES, 2 * SUBLANES, d // 2), _BF16),
                            pltpu.VMEM((bm // SUBLANES, 2 * SUBLANES, d // 2), _BF16),
                            pltpu.SemaphoreType.DMA((n_sub,)), pltpu.SemaphoreType.DMA]),
        out_shape=jax.ShapeDtypeStruct((n_groups, 2 * SUBLANES, d // 2), _BF16),
        compiler_params=pltpu.CompilerParams(
            dimension_semantics=("arbitrary",), vmem_limit_bytes=VMEM_LIMIT),
        name="dispatch",
    )(fill, dst, slot, hn)


def _experts_kernel(nb_ref, r0_ref, nv_ref, xs_ref, wgu_ref, bgu_ref, wdn_ref, bdn_ref, ys_ref,
                    wgu_bf, wdn_bf, xbuf, ybuf, xsem, ysem, ypend, *, n_blocks):
    e = pl.program_id(0)
    n_exp = pl.num_programs(0)
    bg = xbuf.shape[1]
    d_exp = wdn_ref.shape[1]
    nb = nb_ref[e]

    @pl.when(e == 0)
    def _():
        ypend[0] = 0
        ypend[1] = 0

    def x_copy(ex, j, slot):
        at = r0_ref[ex] + j * bg
        return pltpu.make_async_copy(xs_ref.at[pl.ds(at, bg)], xbuf.at[slot], xsem.at[slot])

    def y_copy(j, slot):
        at = r0_ref[e] + j * bg
        return pltpu.make_async_copy(ybuf.at[slot], ys_ref.at[pl.ds(at, bg)], ysem.at[slot])

    @pl.when((e == 0) & (nb > 0))
    def _():
        x_copy(e, 0, 0).start()

    @pl.when(nb > 0)
    def _():
        wgu_bf[...] = wgu_ref[0].astype(_BF16)
        wdn_bf[...] = wdn_ref[0].astype(_BF16)

    def block(j, carry):
        slot = lax.rem(j, 2)
        x_copy(e, j, slot).wait()

        @pl.when(j + 1 < nb)
        def _():
            x_copy(e, j + 1, 1 - slot).start()

        @pl.when(ypend[slot] == 1)
        def _():
            y_copy(j, slot).wait()

        xb = _unfold_groups(xbuf[slot])
        gu = _dot(xb, wgu_bf[...]) + bgu_ref[0]
        gate = jnp.minimum(gu[:, 0:d_exp], SWIGLU_LIMIT)
        up = jnp.clip(gu[:, d_exp:], -SWIGLU_LIMIT, SWIGLU_LIMIT)
        act = (up + 1.0) * (gate * _sigmoid(gate * SWIGLU_ALPHA))
        ybuf[slot] = _fold_groups(_dot(act.astype(_BF16), wdn_bf[...]) + bdn_ref[0])
        y_copy(j, slot).start()
        ypend[slot] = 1
        return carry

    lax.fori_loop(0, nb, block, 0)

    nxt = jnp.minimum(e + 1, n_exp - 1)

    @pl.when((e + 1 < n_exp) & (nb_ref[nxt] > 0))
    def _():
        x_copy(nxt, 0, 0).start()

    @pl.when(e == n_exp - 1)
    def _():
        for slot in range(2):
            @pl.when(ypend[slot] == 1)
            def _():
                y_copy(0, slot).wait()

        ybuf[0] = jnp.zeros(ybuf.shape[1:], _BF16)

        def start(q, carry):
            pltpu.make_async_copy(ybuf.at[0], ys_ref.at[pl.ds(q * bg, bg)], ysem.at[0]).start()
            return carry

        def drain(q, carry):
            pltpu.make_async_copy(ybuf.at[0], ys_ref.at[pl.ds(q * bg, bg)], ysem.at[0]).wait()
            return carry

        lax.fori_loop(nv_ref[0], n_blocks, start, 0)
        lax.fori_loop(nv_ref[0], n_blocks, drain, 0)


def _experts_call(nb, r0, n_valid, xs, wgu, bgu, wdn, bdn, bm, n_blocks):
    n_exp, d, d_gu = wgu.shape
    d_exp = wdn.shape[1]
    return pl.pallas_call(
        functools.partial(_experts_kernel, n_blocks=n_blocks),
        grid_spec=pltpu.PrefetchScalarGridSpec(
            num_scalar_prefetch=3,
            grid=(n_exp,),
            in_specs=[pl.BlockSpec(memory_space=pl.ANY),
                      pl.BlockSpec((1, d, d_gu), lambda e, *_: (e, 0, 0)),
                      pl.BlockSpec((1, 1, d_gu), lambda e, *_: (e, 0, 0)),
                      pl.BlockSpec((1, d_exp, d), lambda e, *_: (e, 0, 0)),
                      pl.BlockSpec((1, 1, d), lambda e, *_: (e, 0, 0))],
            out_specs=pl.BlockSpec(memory_space=pl.ANY),
            scratch_shapes=[pltpu.VMEM((d, d_gu), _BF16), pltpu.VMEM((d_exp, d), _BF16),
                            pltpu.VMEM((2, bm // SUBLANES, 2 * SUBLANES, d // 2), _BF16),
                            pltpu.VMEM((2, bm // SUBLANES, 2 * SUBLANES, d // 2), _BF16),
                            pltpu.SemaphoreType.DMA((2,)), pltpu.SemaphoreType.DMA((2,)),
                            pltpu.SMEM((2,), jnp.int32)]),
        out_shape=jax.ShapeDtypeStruct((n_blocks * bm // SUBLANES, 2 * SUBLANES, d // 2), _BF16),
        compiler_params=pltpu.CompilerParams(
            dimension_semantics=("arbitrary",), vmem_limit_bytes=VMEM_LIMIT),
        name="experts",
    )(nb, r0, n_valid, xs, wgu, bgu, wdn, bdn)


def _combine_kernel(src_ref, nxt_ref, slot_ref, gate_ref, h1_ref, gfin_ref, ys_ref, out_ref,
                    ybuf, sem):
    i = pl.program_id(0)
    n_sub, nq = ybuf.shape[:2]
    ns = nq * SUBLANES
    ts = h1_ref.shape[0] // n_sub

    def gather(tab_ref, h_from, h_to):
        def issue(q, carry):
            pltpu.make_async_copy(ys_ref.at[tab_ref[0, h_from, q]], ybuf.at[h_to, q],
                                  sem.at[h_to]).start()
            return carry

        for q in range(nq):
            issue(q, 0)

    @pl.when(i == 0)
    def _():
        gather(src_ref, 0, 0)

    for h in range(n_sub):
        if h + 1 < n_sub:
            gather(src_ref, h + 1, h + 1)
        pltpu.make_async_copy(ys_ref.at[pl.ds(0, nq)], ybuf.at[h], sem.at[h]).wait()

        slot = slot_ref[h * ts:(h + 1) * ts, :]
        gate = gate_ref[h * ts:(h + 1) * ts, :]
        cols = lax.broadcasted_iota(jnp.int32, (ts, ns), 1)
        g = jnp.zeros((ts, ns), _F32)
        for k in range(TOP_K):
            g = jnp.where(cols == slot[:, k:k + 1], gate[:, k:k + 1], g)
        yb = _unfold_groups(ybuf[h])
        if h + 1 == n_sub:
            @pl.when(i < pl.num_programs(0) - 1)
            def _():
                gather(nxt_ref, 0, 0)
        acc = h1_ref[h * ts:(h + 1) * ts, :] + _dot(g.astype(_BF16), yb)
        out_ref[h * ts:(h + 1) * ts, :] = _rms(acc, gfin_ref[...])


def _combine_call(src, slot_t, gate_t, h1, gfin, ys, tq, ns):
    n_tok, d = h1.shape
    n_steps, n_sub, nq = src.shape
    table = lambda f: pl.BlockSpec((1, n_sub, nq), f, memory_space=pltpu.SMEM)
    return pl.pallas_call(
        _combine_kernel,
        grid=(n_steps,),
        in_specs=[table(lambda i: (i, 0, 0)),
                  table(lambda i: (jnp.minimum(i + 1, n_steps - 1), 0, 0)),
                  pl.BlockSpec((tq, TOP_K), lambda i: (i, 0)),
                  pl.BlockSpec((tq, TOP_K), lambda i: (i, 0)),
                  pl.BlockSpec((tq, d), lambda i: (i, 0)),
                  pl.BlockSpec((1, d), lambda i: (0, 0)),
                  pl.BlockSpec(memory_space=pl.ANY)],
        out_specs=pl.BlockSpec((tq, d), lambda i: (i, 0)),
        out_shape=jax.ShapeDtypeStruct((n_tok, d), _F32),
        scratch_shapes=[pltpu.VMEM((n_sub, ns // SUBLANES, 2 * SUBLANES, d // 2), _BF16),
                        pltpu.SemaphoreType.DMA((n_sub,))],
        compiler_params=pltpu.CompilerParams(
            dimension_semantics=("arbitrary",), vmem_limit_bytes=VMEM_LIMIT),
        name="combine",
    )(src, src, slot_t, gate_t, h1, gfin, ys)


def kernel(x, meta_tokens, norm_mix_g, w_in, b_in, w_pool_grp, pool_scale, w_dwconv, b_dwconv, conv_ln_g, conv_ln_b, w_pw2, b_pw2, w_out, norm_ffn_g, router_w, router_b, w_gate_up, b_gate_up, w_down, b_down, norm_final_g):
    bsz, seq, d = x.shape
    assert w_in.shape[0] == 1, "one layer"
    tq = min(MIX_TILE, seq)
    ts = min(SORT_TILE, tq)
    n_sub = tq // ts
    bm = EXPERT_BLOCK
    n_tok = bsz * seq
    n_sort = n_tok // ts
    ns = -(-(ts * TOP_K + N_EXPERTS * (SUBLANES - 1)) // LANES) * LANES
    nq = ns // SUBLANES
    n_main = (n_tok * TOP_K + n_sort * N_EXPERTS * (SUBLANES - 1)) // bm + N_EXPERTS
    n_spill = -(-(n_sub * ns) // bm)
    i32 = jnp.int32

    row = lambda a: a.reshape(1, -1)
    wr_t = router_w[0].T
    wr_hi = wr_t.astype(_BF16)
    wr_lo = (wr_t - wr_hi.astype(_F32)).astype(_BF16)
    tri = (jnp.arange(ts)[:, None] < jnp.arange(ts)[None, :]).astype(_BF16)
    lmat = (jnp.arange(N_EXPERTS)[None, :] < jnp.arange(N_EXPERTS)[:, None]).astype(_BF16)

    h1, hn, gate, slot, segend, off, tot = _mixer_call(
        x, meta_tokens, row(norm_mix_g[0]), w_in[0].astype(_BF16), row(b_in[0]),
        w_pool_grp[0].astype(_BF16), row(pool_scale[0]), w_dwconv[0], row(b_dwconv[0]),
        row(conv_ln_g[0]), row(conv_ln_b[0]), w_pw2[0].astype(_BF16), row(b_pw2[0]),
        w_out[0].astype(_BF16), row(norm_ffn_g[0]), jnp.concatenate([wr_hi, wr_lo], axis=0),
        router_b[0].reshape(N_EXPERTS, 1), tri, lmat, tq)

    segend = segend[:, :, 0].astype(i32)
    off = off[:, :, 0].astype(i32)
    total = tot[:, 0].astype(i32)
    padded = (total + bm - 1) // bm * bm
    pend = jnp.cumsum(padded)
    pstart = pend - padded
    n_valid = pend[-1] // bm
    fill = jnp.stack([(pstart + total) // SUBLANES, pend // SUBLANES,
                      jnp.full_like(pend, n_valid)]).astype(i32)

    q_row = jnp.arange(nq, dtype=i32) * SUBLANES
    e_q = jnp.sum((segend[:, None, :] <= q_row[None, :, None]).astype(i32), axis=-1)
    used = q_row[None, :] < segend[:, -1:]
    hot = jnp.minimum(e_q, N_EXPERTS - 1)[..., None] == jnp.arange(N_EXPERTS, dtype=i32)
    to_row = jnp.sum(jnp.where(hot, (pstart[None, :] + off)[:, None, :], 0), axis=-1) + q_row
    spill = n_main * bm + (jnp.arange(n_sort, dtype=i32) % n_sub)[:, None] * ns + q_row
    dst = (jnp.where(used, to_row, spill) // SUBLANES).reshape(n_sort // n_sub, n_sub, nq)
    src = (jnp.where(used, to_row, 0) // SUBLANES).reshape(n_sort // n_sub, n_sub, nq)

    slot_t = slot.transpose(0, 2, 1).reshape(n_tok, TOP_K)
    gate_t = gate.transpose(0, 2, 1).reshape(n_tok, TOP_K)

    xs = _dispatch_call(fill, dst, slot, hn, (n_main + n_spill) * bm // SUBLANES, tq, ns, bm)
    ys = _experts_call((padded // bm).astype(i32), (pstart // SUBLANES).astype(i32),
                       n_valid.reshape(1).astype(i32), xs,
                       w_gate_up[0], b_gate_up[0].reshape(N_EXPERTS, 1, -1),
                       w_down[0], b_down[0].reshape(N_EXPERTS, 1, -1), bm, n_main)
    out = _combine_call(src, slot_t, gate_t, h1, row(norm_final_g), ys, tq, ns)
    return out.reshape(bsz, seq, d)
```

```python
import functools

import jax
import jax.numpy as jnp
from jax import lax
from jax.experimental import pallas as pl
from jax.experimental.pallas import tpu as pltpu

N_META = 16
POOL_WINDOWS = (2, 4, 8, 16)
POOL_GROUP_DIM = 128
POOL_OUT_DIM = 256
POOL_WIDTH = POOL_GROUP_DIM * len(POOL_WINDOWS)
CONV_WIDTH = 512
CONV_KERNEL = 31
N_EXPERTS = 32
TOP_K = 4
SWIGLU_LIMIT = 7.0
SWIGLU_ALPHA = 1.702
RMS_EPS = 1e-5
LN_EPS = 1e-5

LANES = 128
SUBLANES = 8

MIX_TILE = 512
POOL_HALO = 16
CONV_HALO = 32
CONV_ROWS = 64
SORT_TILE = 256
EXPERT_BLOCK = 512
VMEM_LIMIT = 56 * 1024 * 1024

_F32 = jnp.float32
_BF16 = jnp.bfloat16


def _rms(x, g):
    return x * lax.rsqrt(jnp.mean(x * x, axis=-1, keepdims=True) + RMS_EPS) * g


def _sigmoid(x):
    return 1.0 / (1.0 + jnp.exp(-x))


def _dot(a, b):
    return jnp.dot(a, b, preferred_element_type=_F32)


def _fold_groups(v):
    n, d = v.shape
    left = v[:, :d // 2].reshape(n // SUBLANES, SUBLANES, d // 2)
    right = v[:, d // 2:].reshape(n // SUBLANES, SUBLANES, d // 2)
    return jnp.concatenate([left, right], axis=1).astype(_BF16)


def _unfold_groups(w):
    g, _, half = w.shape
    f = w.astype(_F32)
    left = f[:, :SUBLANES, :].reshape(g * SUBLANES, half)
    right = f[:, SUBLANES:, :].reshape(g * SUBLANES, half)
    return jnp.concatenate([left, right], axis=-1).astype(_BF16)


def _mixer_kernel(x_ref, meta_ref, gmix_ref, win_ref, bin_ref, wpool_ref, pscale_ref,
                  kdw_ref, bdw_ref, lng_ref, lnb_ref, wpw2_ref, bpw2_ref, wout_ref,
                  gffn_ref, wr_ref, rb_ref, tri_ref, lmat_ref,
                  h1_ref, hn_ref, gate_ref, slot_ref, segend_ref, off_ref, tot_ref,
                  ubuf, abuf, base, hn32):
    b = pl.program_id(0)
    j = pl.program_id(1)
    tq = x_ref.shape[1]
    c0 = POOL_WIDTH
    c1 = c0 + CONV_WIDTH
    c2 = c1 + CONV_WIDTH
    d_model = x_ref.shape[2]
    c3 = c2 + d_model
    n_slab = CONV_WIDTH // LANES

    @pl.when((b == 0) & (j == 0))
    def _():
        base[...] = jnp.zeros_like(base)

    @pl.when(j == 0)
    def _():
        xm = _rms(meta_ref[...], gmix_ref[...]).astype(_BF16)
        pm = _dot(xm, win_ref[:, 0:c2]) + bin_ref[:, 0:c2]
        um = pm[:, 0:c0]
        am = pm[:, c0:c1] * _sigmoid(pm[:, c1:c2])
        for s in range(n_slab):
            ubuf[s, 0:POOL_HALO, :] = um[:, s * LANES:(s + 1) * LANES]
            abuf[s, 0:CONV_HALO - N_META, :] = jnp.zeros((CONV_HALO - N_META, LANES), _F32)
            abuf[s, CONV_HALO - N_META:CONV_HALO, :] = am[:, s * LANES:(s + 1) * LANES]

    ts = tri_ref.shape[0]
    n_sub = tq // ts
    nt_dims = (((1,), (1,)), ((), ()))
    eidx = lax.broadcasted_iota(jnp.int32, (N_EXPERTS, ts), 0).astype(_F32)

    def proj_stage(h):
        r0 = h * ts
        xn = _rms(x_ref[0, r0:r0 + ts, :], gmix_ref[...]).astype(_BF16)
        u = _dot(xn, win_ref[:, 0:c0]) + bin_ref[:, 0:c0]
        v = _dot(xn, win_ref[:, c0:c1]) + bin_ref[:, c0:c1]
        gt = _dot(xn, win_ref[:, c1:c2]) + bin_ref[:, c1:c2]
        a = v * _sigmoid(gt)
        for s in range(n_slab):
            ubuf[s, POOL_HALO + r0:POOL_HALO + r0 + ts, :] = u[:, s * LANES:(s + 1) * LANES]
            abuf[s, CONV_HALO + r0:CONV_HALO + r0 + ts, :] = a[:, s * LANES:(s + 1) * LANES]
        return xn

    def conv_stage(h):
        r0 = h * ts
        conv = []
        for s in range(n_slab):
            lo = s * LANES
            rows = []
            for rc in range(r0, r0 + ts, CONV_ROWS):
                acc = jnp.broadcast_to(bdw_ref[:, lo:lo + LANES], (CONV_ROWS, LANES))
                for tap in range(CONV_KERNEL):
                    start = CONV_HALO - (CONV_KERNEL - 1) + tap + rc
                    acc = acc + kdw_ref[tap:tap + 1, lo:lo + LANES] * abuf[s, start:start + CONV_ROWS, :]
                rows.append(acc)
            conv.append(jnp.concatenate(rows, axis=0))
        return jnp.concatenate(conv, axis=-1)

    def mix_stage(h, c, xn):
        r0 = h * ts
        g_a = _dot(xn, win_ref[:, c2:c3]) + bin_ref[:, c2:c3]
        g_b = _dot(xn, win_ref[:, c3:]) + bin_ref[:, c3:]
        ya = []
        for g, w in enumerate(POOL_WINDOWS):
            ug = ubuf[g, POOL_HALO + r0:POOL_HALO + r0 + ts, :]
            acc = ug
            for back in range(1, w):
                acc = acc + ubuf[g, POOL_HALO + r0 - back:POOL_HALO + r0 - back + ts, :]
            dg = acc * (1.0 / w) - ug
            ya.append(_dot(dg.astype(_BF16), wpool_ref[g]))
        y_a = jnp.concatenate(ya, axis=-1) * pscale_ref[...]

        mu = jnp.mean(c, axis=-1, keepdims=True)
        cc = c - mu
        var = jnp.mean(cc * cc, axis=-1, keepdims=True)
        cn = cc * lax.rsqrt(var + LN_EPS) * lng_ref[...] + lnb_ref[...]
        sw = cn * _sigmoid(cn)
        y_b = _dot(sw.astype(_BF16), wpw2_ref[...]) + bpw2_ref[...]

        mixed = _sigmoid(g_a) * y_a + _sigmoid(g_b) * y_b
        h1 = x_ref[0, r0:r0 + ts, :] + _dot(mixed.astype(_BF16), wout_ref[...])
        h1_ref[r0:r0 + ts, :] = h1
        hn = _rms(h1, gffn_ref[...])
        hn32[r0:r0 + ts, :] = hn
        hn_ref[r0:r0 + ts, :] = hn.astype(_BF16)

    def route_stage(h):
        r0 = h * ts
        hn = hn32[r0:r0 + ts, :]
        hn_hi = hn.astype(_BF16)
        hn_lo = (hn - hn_hi.astype(_F32)).astype(_BF16)
        l_hi = lax.dot_general(wr_ref[...], hn_hi, nt_dims, preferred_element_type=_F32)
        l_lo = lax.dot_general(wr_ref[0:N_EXPERTS, :], hn_lo, nt_dims, preferred_element_type=_F32)
        work = l_hi[0:N_EXPERTS] + l_hi[N_EXPERTS:2 * N_EXPERTS] + l_lo + rb_ref[...]

        vals, hots = [], []
        for _ in range(TOP_K):
            m = jnp.max(work, axis=0, keepdims=True)
            first = jnp.min(jnp.where(work == m, eidx, float(N_EXPERTS)), axis=0, keepdims=True)
            hot = eidx == first
            vals.append(m)
            hots.append(hot)
            work = jnp.where(hot, -jnp.inf, work)
        exps = [jnp.exp(vk - vals[0]) for vk in vals]
        denom = exps[0] + exps[1] + exps[2] + exps[3]
        gate_ref[0, :, r0:r0 + ts] = jnp.concatenate([ek / denom for ek in exps], axis=0)

        chosen = jnp.where(hots[0] | hots[1] | hots[2] | hots[3], 1.0, 0.0)
        groups = jnp.floor((jnp.sum(chosen, axis=1, keepdims=True) + (SUBLANES - 1)) * (1.0 / SUBLANES))
        groups = jnp.broadcast_to(groups, (N_EXPERTS, LANES))
        seg = _dot(lmat_ref[...], groups.astype(_BF16)) * float(SUBLANES)
        place = seg[:, 0:1] + _dot(chosen.astype(_BF16), tri_ref[...])
        slots = [jnp.sum(jnp.where(m, place, 0.0), axis=0, keepdims=True) for m in hots]
        slot_ref[h] = jnp.concatenate(slots, axis=0).astype(jnp.int32)
        segend_ref[h] = seg + groups * float(SUBLANES)
        off_ref[h] = base[...] - seg
        base[...] = base[...] + groups * float(SUBLANES)

    xns = [proj_stage(h) for h in range(n_sub)]
    for h in range(n_sub):
        mix_stage(h, conv_stage(h), xns[h])
        if h > 0:
            route_stage(h - 1)
    route_stage(n_sub - 1)
    tot_ref[...] = base[...]

    for s in range(n_slab):
        ubuf[s, 0:POOL_HALO, :] = ubuf[s, tq:tq + POOL_HALO, :]
        abuf[s, 0:CONV_HALO, :] = abuf[s, tq:tq + CONV_HALO, :]


def _mixer_call(x, meta, gmix, win, bin_, wpool, pscale, kdw, bdw, lng, lnb, wpw2, bpw2,
                wout, gffn, wr, rb, tri, lmat, tq):
    bsz, seq, d = x.shape
    nj = seq // tq
    n_tok = bsz * seq
    n_tiles = bsz * nj
    ts = tri.shape[0]
    n_sub = tq // ts

    def full(arr):
        nd = arr.ndim
        return pl.BlockSpec(arr.shape, lambda b, j, _n=nd: (0,) * _n)

    def tile3(k, w):
        return pl.BlockSpec((k, TOP_K, w), lambda b, j: (b * nj + j, 0, 0))

    per_expert = pl.BlockSpec((n_sub, N_EXPERTS, LANES), lambda b, j: (b * nj + j, 0, 0))
    rows = pl.BlockSpec((tq, d), lambda b, j: (b * nj + j, 0))
    consts = (meta, gmix, win, bin_, wpool, pscale, kdw, bdw, lng, lnb, wpw2, bpw2, wout,
              gffn, wr, rb, tri, lmat)
    return pl.pallas_call(
        _mixer_kernel,
        grid=(bsz, nj),
        in_specs=[pl.BlockSpec((1, tq, d), lambda b, j: (b, j, 0))] + [full(c) for c in consts],
        out_specs=[rows, rows, tile3(1, tq), tile3(n_sub, ts), per_expert, per_expert,
                   pl.BlockSpec((N_EXPERTS, LANES), lambda b, j: (0, 0))],
        out_shape=[jax.ShapeDtypeStruct((n_tok, d), _F32),
                   jax.ShapeDtypeStruct((n_tok, d), _BF16),
                   jax.ShapeDtypeStruct((n_tiles, TOP_K, tq), _F32),
                   jax.ShapeDtypeStruct((n_tiles * n_sub, TOP_K, ts), jnp.int32),
                   jax.ShapeDtypeStruct((n_tiles * n_sub, N_EXPERTS, LANES), _F32),
                   jax.ShapeDtypeStruct((n_tiles * n_sub, N_EXPERTS, LANES), _F32),
                   jax.ShapeDtypeStruct((N_EXPERTS, LANES), _F32)],
        scratch_shapes=[pltpu.VMEM((CONV_WIDTH // LANES, POOL_HALO + tq, LANES), _F32),
                        pltpu.VMEM((CONV_WIDTH // LANES, CONV_HALO + tq, LANES), _F32),
                        pltpu.VMEM((N_EXPERTS, LANES), _F32),
                        pltpu.VMEM((tq, d), _F32)],
        compiler_params=pltpu.CompilerParams(
            dimension_semantics=("arbitrary", "arbitrary"), vmem_limit_bytes=VMEM_LIMIT),
        name="mixer",
    )(x, *consts)


def _dispatch_kernel(fill_ref, dst_ref, slot_ref, hn_ref, xs_ref, sbuf, zblk, sem, zsem, *, bm):
    i = pl.program_id(0)
    n_sub, _, ts = slot_ref.shape
    nq = sbuf.shape[1]
    ns = nq * SUBLANES
    bg = bm // SUBLANES
    n_blocks = xs_ref.shape[0] // bg

    def zero_fill(lo, hi, size, wait):
        def step(q, carry):
            cp = pltpu.make_async_copy(zblk.at[pl.ds(0, size)], xs_ref.at[pl.ds(q * size, size)], zsem)
            cp.wait() if wait else cp.start()
            return carry

        lax.fori_loop(lo, hi, step, 0)

    @pl.when(i == 0)
    def _():
        zblk[...] = jnp.zeros_like(zblk)
        for wait in (False, True):
            for e in range(N_EXPERTS):
                zero_fill(fill_ref[0, e], fill_ref[1, e], 1, wait)
            zero_fill(fill_ref[2, 0], n_blocks, bg, wait)

    for h in range(n_sub):
        @pl.when(i > 0)
        def _():
            pltpu.make_async_copy(sbuf.at[h], xs_ref.at[pl.ds(0, nq)], sem.at[h]).wait()

        slot = slot_ref[h]
        rows = lax.broadcasted_iota(jnp.int32, (ns, ts), 0)
        sel = jnp.zeros((ns, ts), _F32)
        for k in range(TOP_K):
            sel = jnp.where(rows == slot[k:k + 1, :], 1.0, sel)
        sbuf[h] = _fold_groups(_dot(sel.astype(_BF16), hn_ref[h * ts:(h + 1) * ts, :]))

        def issue(q, carry):
            pltpu.make_async_copy(sbuf.at[h, q], xs_ref.at[dst_ref[0, h, q]], sem.at[h]).start()
            return carry

        for q in range(nq):
            issue(q, 0)

    @pl.when(i == pl.num_programs(0) - 1)
    def _():
        for h in range(n_sub):
            pltpu.make_async_copy(sbuf.at[h], xs_ref.at[pl.ds(0, nq)], sem.at[h]).wait()


def _dispatch_call(fill, dst, slot, hn, n_groups, tq, ns, bm):
    n_tok, d = hn.shape
    n_sub, ts = dst.shape[1], slot.shape[2]
    return pl.pallas_call(
        functools.partial(_dispatch_kernel, bm=bm),
        grid_spec=pltpu.PrefetchScalarGridSpec(
            num_scalar_prefetch=1,
            grid=(n_tok // tq,),
            in_specs=[pl.BlockSpec((1, n_sub, ns // SUBLANES), lambda i, f: (i, 0, 0),
                                   memory_space=pltpu.SMEM),
                      pl.BlockSpec((n_sub, TOP_K, ts), lambda i, f: (i, 0, 0)),
                      pl.BlockSpec((tq, d), lambda i, f: (i, 0))],
            out_specs=pl.BlockSpec(memory_space=pl.ANY),
            scratch_shapes=[pltpu.VMEM((n_sub, ns // SUBLANES, 2 * SUBLANES, d // 2), _BF16),
                            pltpu.VMEM((bm // SUBLANES, 2 * SUBLANES, d // 2), _BF16),
                            pltpu.SemaphoreType.DMA((n_sub,)), pltpu.SemaphoreType.DMA]),
        out_shape=jax.ShapeDtypeStruct((n_groups, 2 * SUBLANES, d // 2), _BF16),
        compiler_params=pltpu.CompilerParams(
            dimension_semantics=("arbitrary",), vmem_limit_bytes=VMEM_LIMIT),
        name="dispatch",
    )(fill, dst, slot, hn)


def _experts_kernel(nb_ref, tl_ref, r0_ref, nv_ref, xs_ref, wgu_ref, bgu_ref, wdn_ref, bdn_ref,
                    ys_ref, wgu_bf, wdn_bf, xbuf, ybuf, xtail, ytail, xsem, ysem, tsem, ypend,
                    *, n_half):
    e = pl.program_id(0)
    n_exp = pl.num_programs(0)
    bg = xbuf.shape[1]
    hg = xtail.shape[0]
    d_exp = wdn_ref.shape[1]
    nb = nb_ref[e]
    has_tail = tl_ref[e] == 1

    @pl.when(e == 0)
    def _():
        ypend[0] = 0
        ypend[1] = 0
        ypend[2] = 0

    def x_copy(ex, j, slot):
        at = r0_ref[ex] + j * bg
        return pltpu.make_async_copy(xs_ref.at[pl.ds(at, bg)], xbuf.at[slot], xsem.at[slot])

    def y_copy(j, slot):
        at = r0_ref[e] + j * bg
        return pltpu.make_async_copy(ybuf.at[slot], ys_ref.at[pl.ds(at, bg)], ysem.at[slot])

    def tail_x():
        return pltpu.make_async_copy(xs_ref.at[pl.ds(r0_ref[e] + nb * bg, hg)], xtail, tsem.at[0])

    def tail_y():
        return pltpu.make_async_copy(ytail, ys_ref.at[pl.ds(r0_ref[e] + nb * bg, hg)], tsem.at[1])

    def mlp(xb):
        gu = _dot(xb, wgu_bf[...]) + bgu_ref[0]
        gate = jnp.minimum(gu[:, 0:d_exp], SWIGLU_LIMIT)
        up = jnp.clip(gu[:, d_exp:], -SWIGLU_LIMIT, SWIGLU_LIMIT)
        act = (up + 1.0) * (gate * _sigmoid(gate * SWIGLU_ALPHA))
        return _fold_groups(_dot(act.astype(_BF16), wdn_bf[...]) + bdn_ref[0])

    @pl.when((e == 0) & (nb > 0))
    def _():
        x_copy(e, 0, 0).start()

    @pl.when(has_tail)
    def _():
        tail_x().start()

    @pl.when((nb > 0) | has_tail)
    def _():
        wgu_bf[...] = wgu_ref[0].astype(_BF16)
        wdn_bf[...] = wdn_ref[0].astype(_BF16)

    def block(j, carry):
        slot = lax.rem(j, 2)
        x_copy(e, j, slot).wait()

        @pl.when(j + 1 < nb)
        def _():
            x_copy(e, j + 1, 1 - slot).start()

        @pl.when(ypend[slot] == 1)
        def _():
            y_copy(j, slot).wait()

        ybuf[slot] = mlp(_unfold_groups(xbuf[slot]))
        y_copy(j, slot).start()
        ypend[slot] = 1
        return carry

    lax.fori_loop(0, nb, block, 0)

    nxt = jnp.minimum(e + 1, n_exp - 1)

    @pl.when((e + 1 < n_exp) & (nb_ref[nxt] > 0))
    def _():
        x_copy(nxt, 0, 0).start()

    @pl.when(has_tail)
    def _():
        tail_x().wait()

        @pl.when(ypend[2] == 1)
        def _():
            tail_y().wait()

        ytail[...] = mlp(_unfold_groups(xtail[...]))
        tail_y().start()
        ypend[2] = 1

    @pl.when(e == n_exp - 1)
    def _():
        for slot in range(2):
            @pl.when(ypend[slot] == 1)
            def _():
                y_copy(0, slot).wait()

        @pl.when(ypend[2] == 1)
        def _():
            tail_y().wait()

        ytail[...] = jnp.zeros(ytail.shape, _BF16)

        def start(q, carry):
            pltpu.make_async_copy(ytail, ys_ref.at[pl.ds(q * hg, hg)], tsem.at[1]).start()
            return carry

        def drain(q, carry):
            pltpu.make_async_copy(ytail, ys_ref.at[pl.ds(q * hg, hg)], tsem.at[1]).wait()
            return carry

        lax.fori_loop(nv_ref[0], n_half, start, 0)
        lax.fori_loop(nv_ref[0], n_half, drain, 0)


def _experts_call(nb, tail, r0, n_valid, xs, wgu, bgu, wdn, bdn, bm, n_half):
    n_exp, d, d_gu = wgu.shape
    d_exp = wdn.shape[1]
    group_tile = (2 * SUBLANES, d // 2)
    return pl.pallas_call(
        functools.partial(_experts_kernel, n_half=n_half),
        grid_spec=pltpu.PrefetchScalarGridSpec(
            num_scalar_prefetch=4,
            grid=(n_exp,),
            in_specs=[pl.BlockSpec(memory_space=pl.ANY),
                      pl.BlockSpec((1, d, d_gu), lambda e, *_: (e, 0, 0)),
                      pl.BlockSpec((1, 1, d_gu), lambda e, *_: (e, 0, 0)),
                      pl.BlockSpec((1, d_exp, d), lambda e, *_: (e, 0, 0)),
                      pl.BlockSpec((1, 1, d), lambda e, *_: (e, 0, 0))],
            out_specs=pl.BlockSpec(memory_space=pl.ANY),
            scratch_shapes=[pltpu.VMEM((d, d_gu), _BF16), pltpu.VMEM((d_exp, d), _BF16),
                            pltpu.VMEM((2, bm // SUBLANES) + group_tile, _BF16),
                            pltpu.VMEM((2, bm // SUBLANES) + group_tile, _BF16),
                            pltpu.VMEM((bm // 2 // SUBLANES,) + group_tile, _BF16),
                            pltpu.VMEM((bm // 2 // SUBLANES,) + group_tile, _BF16),
                            pltpu.SemaphoreType.DMA((2,)), pltpu.SemaphoreType.DMA((2,)),
                            pltpu.SemaphoreType.DMA((2,)), pltpu.SMEM((3,), jnp.int32)]),
        out_shape=jax.ShapeDtypeStruct((n_half * (bm // 2) // SUBLANES,) + group_tile, _BF16),
        compiler_params=pltpu.CompilerParams(
            dimension_semantics=("arbitrary",), vmem_limit_bytes=VMEM_LIMIT),
        name="experts",
    )(nb, tail, r0, n_valid, xs, wgu, bgu, wdn, bdn)


def _combine_kernel(src_ref, nxt_ref, slot_ref, gate_ref, h1_ref, gfin_ref, ys_ref, out_ref,
                    ybuf, sem):
    i = pl.program_id(0)
    n_sub, nq = ybuf.shape[:2]
    ns = nq * SUBLANES
    ts = h1_ref.shape[0] // n_sub

    def gather(tab_ref, h_from, h_to):
        def issue(q, carry):
            pltpu.make_async_copy(ys_ref.at[tab_ref[0, h_from, q]], ybuf.at[h_to, q],
                                  sem.at[h_to]).start()
            return carry

        for q in range(nq):
            issue(q, 0)

    @pl.when(i == 0)
    def _():
        gather(src_ref, 0, 0)

    for h in range(n_sub):
        if h + 1 < n_sub:
            gather(src_ref, h + 1, h + 1)
        pltpu.make_async_copy(ys_ref.at[pl.ds(0, nq)], ybuf.at[h], sem.at[h]).wait()

        slot = slot_ref[h * ts:(h + 1) * ts, :]
        gate = gate_ref[h * ts:(h + 1) * ts, :]
        cols = lax.broadcasted_iota(jnp.int32, (ts, ns), 1)
        g = jnp.zeros((ts, ns), _F32)
        for k in range(TOP_K):
            g = jnp.where(cols == slot[:, k:k + 1], gate[:, k:k + 1], g)
        yb = _unfold_groups(ybuf[h])
        if h + 1 == n_sub:
            @pl.when(i < pl.num_programs(0) - 1)
            def _():
                gather(nxt_ref, 0, 0)
        acc = h1_ref[h * ts:(h + 1) * ts, :] + _dot(g.astype(_BF16), yb)
        out_ref[h * ts:(h + 1) * ts, :] = _rms(acc, gfin_ref[...])


def _combine_call(src, slot_t, gate_t, h1, gfin, ys, tq, ns):
    n_tok, d = h1.shape
    n_steps, n_sub, nq = src.shape
    table = lambda f: pl.BlockSpec((1, n_sub, nq), f, memory_space=pltpu.SMEM)
    return pl.pallas_call(
        _combine_kernel,
        grid=(n_steps,),
        in_specs=[table(lambda i: (i, 0, 0)),
                  table(lambda i: (jnp.minimum(i + 1, n_steps - 1), 0, 0)),
                  pl.BlockSpec((tq, TOP_K), lambda i: (i, 0)),
                  pl.BlockSpec((tq, TOP_K), lambda i: (i, 0)),
                  pl.BlockSpec((tq, d), lambda i: (i, 0)),
                  pl.BlockSpec((1, d), lambda i: (0, 0)),
                  pl.BlockSpec(memory_space=pl.ANY)],
        out_specs=pl.BlockSpec((tq, d), lambda i: (i, 0)),
        out_shape=jax.ShapeDtypeStruct((n_tok, d), _F32),
        scratch_shapes=[pltpu.VMEM((n_sub, ns // SUBLANES, 2 * SUBLANES, d // 2), _BF16),
                        pltpu.SemaphoreType.DMA((n_sub,))],
        compiler_params=pltpu.CompilerParams(
            dimension_semantics=("arbitrary",), vmem_limit_bytes=VMEM_LIMIT),
        name="combine",
    )(src, src, slot_t, gate_t, h1, gfin, ys)


def kernel(x, meta_tokens, norm_mix_g, w_in, b_in, w_pool_grp, pool_scale, w_dwconv, b_dwconv, conv_ln_g, conv_ln_b, w_pw2, b_pw2, w_out, norm_ffn_g, router_w, router_b, w_gate_up, b_gate_up, w_down, b_down, norm_final_g):
    bsz, seq, d = x.shape
    assert w_in.shape[0] == 1, "one layer"
    tq = min(MIX_TILE, seq)
    ts = min(SORT_TILE, tq)
    n_sub = tq // ts
    bm = EXPERT_BLOCK
    n_tok = bsz * seq
    n_sort = n_tok // ts
    ns = -(-(ts * TOP_K + N_EXPERTS * (SUBLANES - 1)) // LANES) * LANES
    nq = ns // SUBLANES
    hb = bm // 2
    n_main = (n_tok * TOP_K + n_sort * N_EXPERTS * (SUBLANES - 1)) // hb + N_EXPERTS
    n_spill = -(-(n_sub * ns) // hb)
    i32 = jnp.int32

    row = lambda a: a.reshape(1, -1)
    wr_t = router_w[0].T
    wr_hi = wr_t.astype(_BF16)
    wr_lo = (wr_t - wr_hi.astype(_F32)).astype(_BF16)
    tri = (jnp.arange(ts)[:, None] < jnp.arange(ts)[None, :]).astype(_BF16)
    lmat = (jnp.arange(N_EXPERTS)[None, :] < jnp.arange(N_EXPERTS)[:, None]).astype(_BF16)

    h1, hn, gate, slot, segend, off, tot = _mixer_call(
        x, meta_tokens, row(norm_mix_g[0]), w_in[0].astype(_BF16), row(b_in[0]),
        w_pool_grp[0].astype(_BF16), row(pool_scale[0]), w_dwconv[0], row(b_dwconv[0]),
        row(conv_ln_g[0]), row(conv_ln_b[0]), w_pw2[0].astype(_BF16), row(b_pw2[0]),
        w_out[0].astype(_BF16), row(norm_ffn_g[0]), jnp.concatenate([wr_hi, wr_lo], axis=0),
        router_b[0].reshape(N_EXPERTS, 1), tri, lmat, tq)

    segend = segend[:, :, 0].astype(i32)
    off = off[:, :, 0].astype(i32)
    total = tot[:, 0].astype(i32)
    padded = (total + hb - 1) // hb * hb
    pend = jnp.cumsum(padded)
    pstart = pend - padded
    n_valid = pend[-1] // hb
    fill = jnp.stack([(pstart + total) // SUBLANES, pend // SUBLANES,
                      jnp.full_like(pend, n_valid)]).astype(i32)

    q_row = jnp.arange(nq, dtype=i32) * SUBLANES
    e_q = jnp.sum((segend[:, None, :] <= q_row[None, :, None]).astype(i32), axis=-1)
    used = q_row[None, :] < segend[:, -1:]
    hot = jnp.minimum(e_q, N_EXPERTS - 1)[..., None] == jnp.arange(N_EXPERTS, dtype=i32)
    to_row = jnp.sum(jnp.where(hot, (pstart[None, :] + off)[:, None, :], 0), axis=-1) + q_row
    spill = n_main * hb + (jnp.arange(n_sort, dtype=i32) % n_sub)[:, None] * ns + q_row
    dst = (jnp.where(used, to_row, spill) // SUBLANES).reshape(n_sort // n_sub, n_sub, nq)
    src = (jnp.where(used, to_row, 0) // SUBLANES).reshape(n_sort // n_sub, n_sub, nq)

    slot_t = slot.transpose(0, 2, 1).reshape(n_tok, TOP_K)
    gate_t = gate.transpose(0, 2, 1).reshape(n_tok, TOP_K)

    xs = _dispatch_call(fill, dst, slot, hn, (n_main + n_spill) * hb // SUBLANES, tq, ns, hb)
    ys = _experts_call((padded // bm).astype(i32), (padded // hb % 2).astype(i32),
                       (pstart // SUBLANES).astype(i32), n_valid.reshape(1).astype(i32), xs,
                       w_gate_up[0], b_gate_up[0].reshape(N_EXPERTS, 1, -1),
                       w_down[0], b_down[0].reshape(N_EXPERTS, 1, -1), bm, n_main)
    out = _combine_call(src, slot_t, gate_t, h1, row(norm_final_g), ys, tq, ns)
    return out.reshape(bsz, seq, d)
```

```python
import functools

import jax
import jax.numpy as jnp
from jax import lax
from jax.experimental import pallas as pl
from jax.experimental.pallas import tpu as pltpu

N_META = 16
POOL_WINDOWS = (2, 4, 8, 16)
POOL_GROUP_DIM = 128
POOL_OUT_DIM = 256
POOL_WIDTH = POOL_GROUP_DIM * len(POOL_WINDOWS)
CONV_WIDTH = 512
CONV_KERNEL = 31
N_EXPERTS = 32
TOP_K = 4
SWIGLU_LIMIT = 7.0
SWIGLU_ALPHA = 1.702
RMS_EPS = 1e-5
LN_EPS = 1e-5

LANES = 128
SUBLANES = 8

MIX_TILE = 512
POOL_HALO = 16
CONV_HALO = 32
CONV_ROWS = 64
SORT_TILE = 256
ROW_STEP = 1024
EXPERT_BLOCK = 512
VMEM_LIMIT = 56 * 1024 * 1024

_F32 = jnp.float32
_BF16 = jnp.bfloat16


def _rms(x, g):
    return x * lax.rsqrt(jnp.mean(x * x, axis=-1, keepdims=True) + RMS_EPS) * g


def _sigmoid(x):
    return 1.0 / (1.0 + jnp.exp(-x))


def _dot(a, b):
    return jnp.dot(a, b, preferred_element_type=_F32)


def _fold_groups(v):
    n, d = v.shape
    left = v[:, :d // 2].reshape(n // SUBLANES, SUBLANES, d // 2)
    right = v[:, d // 2:].reshape(n // SUBLANES, SUBLANES, d // 2)
    return jnp.concatenate([left, right], axis=1).astype(_BF16)


def _unfold_groups(w):
    g, _, half = w.shape
    f = w.astype(_F32)
    left = f[:, :SUBLANES, :].reshape(g * SUBLANES, half)
    right = f[:, SUBLANES:, :].reshape(g * SUBLANES, half)
    return jnp.concatenate([left, right], axis=-1).astype(_BF16)


def _mixer_kernel(x_ref, meta_ref, gmix_ref, win_ref, bin_ref, wpool_ref, pscale_ref,
                  kdw_ref, bdw_ref, lng_ref, lnb_ref, wpw2_ref, bpw2_ref, wout_ref,
                  gffn_ref, wr_ref, rb_ref, tri_ref, lmat_ref,
                  h1_ref, hn_ref, gate_ref, slot_ref, segend_ref, off_ref, tot_ref,
                  ubuf, abuf, base, hn32):
    b = pl.program_id(0)
    j = pl.program_id(1)
    tq = x_ref.shape[1]
    c0 = POOL_WIDTH
    c1 = c0 + CONV_WIDTH
    c2 = c1 + CONV_WIDTH
    d_model = x_ref.shape[2]
    c3 = c2 + d_model
    n_slab = CONV_WIDTH // LANES

    @pl.when((b == 0) & (j == 0))
    def _():
        base[...] = jnp.zeros_like(base)

    @pl.when(j == 0)
    def _():
        xm = _rms(meta_ref[...], gmix_ref[...]).astype(_BF16)
        pm = _dot(xm, win_ref[:, 0:c2]) + bin_ref[:, 0:c2]
        um = pm[:, 0:c0]
        am = pm[:, c0:c1] * _sigmoid(pm[:, c1:c2])
        for s in range(n_slab):
            ubuf[s, 0:POOL_HALO, :] = um[:, s * LANES:(s + 1) * LANES]
            abuf[s, 0:CONV_HALO - N_META, :] = jnp.zeros((CONV_HALO - N_META, LANES), _F32)
            abuf[s, CONV_HALO - N_META:CONV_HALO, :] = am[:, s * LANES:(s + 1) * LANES]

    ts = tri_ref.shape[0]
    n_sub = tq // ts
    nt_dims = (((1,), (1,)), ((), ()))
    eidx = lax.broadcasted_iota(jnp.int32, (N_EXPERTS, ts), 0).astype(_F32)

    def proj_stage(h):
        r0 = h * ts
        xn = _rms(x_ref[0, r0:r0 + ts, :], gmix_ref[...]).astype(_BF16)
        u = _dot(xn, win_ref[:, 0:c0]) + bin_ref[:, 0:c0]
        v = _dot(xn, win_ref[:, c0:c1]) + bin_ref[:, c0:c1]
        gt = _dot(xn, win_ref[:, c1:c2]) + bin_ref[:, c1:c2]
        a = v * _sigmoid(gt)
        for s in range(n_slab):
            ubuf[s, POOL_HALO + r0:POOL_HALO + r0 + ts, :] = u[:, s * LANES:(s + 1) * LANES]
            abuf[s, CONV_HALO + r0:CONV_HALO + r0 + ts, :] = a[:, s * LANES:(s + 1) * LANES]
        return xn

    def conv_stage(h):
        r0 = h * ts
        conv = []
        for s in range(n_slab):
            lo = s * LANES
            rows = []
            for rc in range(r0, r0 + ts, CONV_ROWS):
                acc = jnp.broadcast_to(bdw_ref[:, lo:lo + LANES], (CONV_ROWS, LANES))
                for tap in range(CONV_KERNEL):
                    start = CONV_HALO - (CONV_KERNEL - 1) + tap + rc
                    acc = acc + kdw_ref[tap:tap + 1, lo:lo + LANES] * abuf[s, start:start + CONV_ROWS, :]
                rows.append(acc)
            conv.append(jnp.concatenate(rows, axis=0))
        return jnp.concatenate(conv, axis=-1)

    def mix_stage(h, c, xn):
        r0 = h * ts
        g_a = _dot(xn, win_ref[:, c2:c3]) + bin_ref[:, c2:c3]
        g_b = _dot(xn, win_ref[:, c3:]) + bin_ref[:, c3:]
        ya = []
        for g, w in enumerate(POOL_WINDOWS):
            ug = ubuf[g, POOL_HALO + r0:POOL_HALO + r0 + ts, :]
            acc = ug
            for back in range(1, w):
                acc = acc + ubuf[g, POOL_HALO + r0 - back:POOL_HALO + r0 - back + ts, :]
            dg = acc * (1.0 / w) - ug
            ya.append(_dot(dg.astype(_BF16), wpool_ref[g]))
        y_a = jnp.concatenate(ya, axis=-1) * pscale_ref[...]

        mu = jnp.mean(c, axis=-1, keepdims=True)
        cc = c - mu
        var = jnp.mean(cc * cc, axis=-1, keepdims=True)
        cn = cc * lax.rsqrt(var + LN_EPS) * lng_ref[...] + lnb_ref[...]
        sw = cn * _sigmoid(cn)
        y_b = _dot(sw.astype(_BF16), wpw2_ref[...]) + bpw2_ref[...]

        mixed = _sigmoid(g_a) * y_a + _sigmoid(g_b) * y_b
        h1 = x_ref[0, r0:r0 + ts, :] + _dot(mixed.astype(_BF16), wout_ref[...])
        h1_ref[r0:r0 + ts, :] = h1
        hn = _rms(h1, gffn_ref[...])
        hn32[r0:r0 + ts, :] = hn
        hn_ref[r0:r0 + ts, :] = hn.astype(_BF16)

    def route_stage(h):
        r0 = h * ts
        hn = hn32[r0:r0 + ts, :]
        hn_hi = hn.astype(_BF16)
        hn_lo = (hn - hn_hi.astype(_F32)).astype(_BF16)
        l_hi = lax.dot_general(wr_ref[...], hn_hi, nt_dims, preferred_element_type=_F32)
        l_lo = lax.dot_general(wr_ref[0:N_EXPERTS, :], hn_lo, nt_dims, preferred_element_type=_F32)
        work = l_hi[0:N_EXPERTS] + l_hi[N_EXPERTS:2 * N_EXPERTS] + l_lo + rb_ref[...]

        vals, hots = [], []
        for _ in range(TOP_K):
            m = jnp.max(work, axis=0, keepdims=True)
            first = jnp.min(jnp.where(work == m, eidx, float(N_EXPERTS)), axis=0, keepdims=True)
            hot = eidx == first
            vals.append(m)
            hots.append(hot)
            work = jnp.where(hot, -jnp.inf, work)
        exps = [jnp.exp(vk - vals[0]) for vk in vals]
        denom = exps[0] + exps[1] + exps[2] + exps[3]
        gate_ref[0, :, r0:r0 + ts] = jnp.concatenate([ek / denom for ek in exps], axis=0)

        chosen = jnp.where(hots[0] | hots[1] | hots[2] | hots[3], 1.0, 0.0)
        groups = jnp.floor((jnp.sum(chosen, axis=1, keepdims=True) + (SUBLANES - 1)) * (1.0 / SUBLANES))
        groups = jnp.broadcast_to(groups, (N_EXPERTS, LANES))
        seg = _dot(lmat_ref[...], groups.astype(_BF16)) * float(SUBLANES)
        place = seg[:, 0:1] + _dot(chosen.astype(_BF16), tri_ref[...])
        slots = [jnp.sum(jnp.where(m, place, 0.0), axis=0, keepdims=True) for m in hots]
        slot_ref[h] = jnp.concatenate(slots, axis=0).astype(jnp.int32)
        segend_ref[h] = seg + groups * float(SUBLANES)
        off_ref[h] = base[...] - seg
        base[...] = base[...] + groups * float(SUBLANES)

    xns = [proj_stage(h) for h in range(n_sub)]
    for h in range(n_sub):
        mix_stage(h, conv_stage(h), xns[h])
        if h > 0:
            route_stage(h - 1)
    route_stage(n_sub - 1)
    tot_ref[...] = base[...]

    for s in range(n_slab):
        ubuf[s, 0:POOL_HALO, :] = ubuf[s, tq:tq + POOL_HALO, :]
        abuf[s, 0:CONV_HALO, :] = abuf[s, tq:tq + CONV_HALO, :]


def _mixer_call(x, meta, gmix, win, bin_, wpool, pscale, kdw, bdw, lng, lnb, wpw2, bpw2,
                wout, gffn, wr, rb, tri, lmat, tq):
    bsz, seq, d = x.shape
    nj = seq // tq
    n_tok = bsz * seq
    n_tiles = bsz * nj
    ts = tri.shape[0]
    n_sub = tq // ts

    def full(arr):
        nd = arr.ndim
        return pl.BlockSpec(arr.shape, lambda b, j, _n=nd: (0,) * _n)

    def tile3(k, w):
        return pl.BlockSpec((k, TOP_K, w), lambda b, j: (b * nj + j, 0, 0))

    per_expert = pl.BlockSpec((n_sub, N_EXPERTS, LANES), lambda b, j: (b * nj + j, 0, 0))
    rows = pl.BlockSpec((tq, d), lambda b, j: (b * nj + j, 0))
    consts = (meta, gmix, win, bin_, wpool, pscale, kdw, bdw, lng, lnb, wpw2, bpw2, wout,
              gffn, wr, rb, tri, lmat)
    return pl.pallas_call(
        _mixer_kernel,
        grid=(bsz, nj),
        in_specs=[pl.BlockSpec((1, tq, d), lambda b, j: (b, j, 0))] + [full(c) for c in consts],
        out_specs=[rows, rows, tile3(1, tq), tile3(n_sub, ts), per_expert, per_expert,
                   pl.BlockSpec((N_EXPERTS, LANES), lambda b, j: (0, 0))],
        out_shape=[jax.ShapeDtypeStruct((n_tok, d), _F32),
                   jax.ShapeDtypeStruct((n_tok, d), _BF16),
                   jax.ShapeDtypeStruct((n_tiles, TOP_K, tq), _F32),
                   jax.ShapeDtypeStruct((n_tiles * n_sub, TOP_K, ts), jnp.int32),
                   jax.ShapeDtypeStruct((n_tiles * n_sub, N_EXPERTS, LANES), _F32),
                   jax.ShapeDtypeStruct((n_tiles * n_sub, N_EXPERTS, LANES), _F32),
                   jax.ShapeDtypeStruct((N_EXPERTS, LANES), _F32)],
        scratch_shapes=[pltpu.VMEM((CONV_WIDTH // LANES, POOL_HALO + tq, LANES), _F32),
                        pltpu.VMEM((CONV_WIDTH // LANES, CONV_HALO + tq, LANES), _F32),
                        pltpu.VMEM((N_EXPERTS, LANES), _F32),
                        pltpu.VMEM((tq, d), _F32)],
        compiler_params=pltpu.CompilerParams(
            dimension_semantics=("arbitrary", "arbitrary"), vmem_limit_bytes=VMEM_LIMIT),
        name="mixer",
    )(x, *consts)


def _dispatch_kernel(fill_ref, dst_ref, slot_ref, hn_ref, xs_ref, sbuf, zblk, sem, zsem, *, bm):
    i = pl.program_id(0)
    n_sub, _, ts = slot_ref.shape
    nq = sbuf.shape[1]
    ns = nq * SUBLANES
    bg = bm // SUBLANES
    n_blocks = xs_ref.shape[0] // bg

    def zero_fill(lo, hi, size, wait):
        def step(q, carry):
            cp = pltpu.make_async_copy(zblk.at[pl.ds(0, size)], xs_ref.at[pl.ds(q * size, size)], zsem)
            cp.wait() if wait else cp.start()
            return carry

        lax.fori_loop(lo, hi, step, 0)

    @pl.when(i == 0)
    def _():
        zblk[...] = jnp.zeros_like(zblk)
        for wait in (False, True):
            for e in range(N_EXPERTS):
                zero_fill(fill_ref[0, e], fill_ref[1, e], 1, wait)
            zero_fill(fill_ref[2, 0], n_blocks, bg, wait)

    for h in range(n_sub):
        @pl.when(i > 0)
        def _():
            pltpu.make_async_copy(sbuf.at[h], xs_ref.at[pl.ds(0, nq)], sem.at[h]).wait()

        slot = slot_ref[h]
        rows = lax.broadcasted_iota(jnp.int32, (ns, ts), 0)
        sel = jnp.zeros((ns, ts), _F32)
        for k in range(TOP_K):
            sel = jnp.where(rows == slot[k:k + 1, :], 1.0, sel)
        sbuf[h] = _fold_groups(_dot(sel.astype(_BF16), hn_ref[h * ts:(h + 1) * ts, :]))

        def issue(q, carry):
            pltpu.make_async_copy(sbuf.at[h, q], xs_ref.at[dst_ref[0, h, q]], sem.at[h]).start()
            return carry

        for q in range(nq):
            issue(q, 0)

    @pl.when(i == pl.num_programs(0) - 1)
    def _():
        for h in range(n_sub):
            pltpu.make_async_copy(sbuf.at[h], xs_ref.at[pl.ds(0, nq)], sem.at[h]).wait()


def _dispatch_call(fill, dst, slot, hn, n_groups, tq, ns, bm):
    n_tok, d = hn.shape
    n_sub, ts = dst.shape[1], slot.shape[2]
    return pl.pallas_call(
        functools.partial(_dispatch_kernel, bm=bm),
        grid_spec=pltpu.PrefetchScalarGridSpec(
            num_scalar_prefetch=1,
            grid=(n_tok // tq,),
            in_specs=[pl.BlockSpec((1, n_sub, ns // SUBLANES), lambda i, f: (i, 0, 0),
                                   memory_space=pltpu.SMEM),
                      pl.BlockSpec((n_sub, TOP_K, ts), lambda i, f: (i, 0, 0)),
                      pl.BlockSpec((tq, d), lambda i, f: (i, 0))],
            out_specs=pl.BlockSpec(memory_space=pl.ANY),
            scratch_shapes=[pltpu.VMEM((n_sub, ns // SUBLANES, 2 * SUBLANES, d // 2), _BF16),
                            pltpu.VMEM((bm // SUBLANES, 2 * SUBLANES, d // 2), _BF16),
                            pltpu.SemaphoreType.DMA((n_sub,)), pltpu.SemaphoreType.DMA]),
        out_shape=jax.ShapeDtypeStruct((n_groups, 2 * SUBLANES, d // 2), _BF16),
        compiler_params=pltpu.CompilerParams(
            dimension_semantics=("arbitrary",), vmem_limit_bytes=VMEM_LIMIT),
        name="dispatch",
    )(fill, dst, slot, hn)


def _experts_kernel(nb_ref, tl_ref, r0_ref, nv_ref, xs_ref, wgu_ref, bgu_ref, wdn_ref, bdn_ref,
                    ys_ref, wgu_bf, wdn_bf, xbuf, ybuf, xtail, ytail, xsem, ysem, tsem, ypend,
                    *, n_half):
    e = pl.program_id(0)
    n_exp = pl.num_programs(0)
    bg = xbuf.shape[1]
    hg = xtail.shape[0]
    d_exp = wdn_ref.shape[1]
    nb = nb_ref[e]
    has_tail = tl_ref[e] == 1

    @pl.when(e == 0)
    def _():
        ypend[0] = 0
        ypend[1] = 0
        ypend[2] = 0

    def x_copy(ex, j, slot):
        at = r0_ref[ex] + j * bg
        return pltpu.make_async_copy(xs_ref.at[pl.ds(at, bg)], xbuf.at[slot], xsem.at[slot])

    def y_copy(j, slot):
        at = r0_ref[e] + j * bg
        return pltpu.make_async_copy(ybuf.at[slot], ys_ref.at[pl.ds(at, bg)], ysem.at[slot])

    def tail_x():
        return pltpu.make_async_copy(xs_ref.at[pl.ds(r0_ref[e] + nb * bg, hg)], xtail, tsem.at[0])

    def tail_y():
        return pltpu.make_async_copy(ytail, ys_ref.at[pl.ds(r0_ref[e] + nb * bg, hg)], tsem.at[1])

    def mlp(xb):
        gu = _dot(xb, wgu_bf[...]) + bgu_ref[0]
        gate = jnp.minimum(gu[:, 0:d_exp], SWIGLU_LIMIT)
        up = jnp.clip(gu[:, d_exp:], -SWIGLU_LIMIT, SWIGLU_LIMIT)
        act = (up + 1.0) * (gate * _sigmoid(gate * SWIGLU_ALPHA))
        return _fold_groups(_dot(act.astype(_BF16), wdn_bf[...]) + bdn_ref[0])

    @pl.when((e == 0) & (nb > 0))
    def _():
        x_copy(e, 0, 0).start()

    @pl.when(has_tail)
    def _():
        tail_x().start()

    @pl.when((nb > 0) | has_tail)
    def _():
        wgu_bf[...] = wgu_ref[0].astype(_BF16)
        wdn_bf[...] = wdn_ref[0].astype(_BF16)

    def block(j, carry):
        slot = lax.rem(j, 2)
        x_copy(e, j, slot).wait()

        @pl.when(j + 1 < nb)
        def _():
            x_copy(e, j + 1, 1 - slot).start()

        @pl.when(ypend[slot] == 1)
        def _():
            y_copy(j, slot).wait()

        ybuf[slot] = mlp(_unfold_groups(xbuf[slot]))
        y_copy(j, slot).start()
        ypend[slot] = 1
        return carry

    lax.fori_loop(0, nb, block, 0)

    nxt = jnp.minimum(e + 1, n_exp - 1)

    @pl.when((e + 1 < n_exp) & (nb_ref[nxt] > 0))
    def _():
        x_copy(nxt, 0, 0).start()

    @pl.when(has_tail)
    def _():
        tail_x().wait()

        @pl.when(ypend[2] == 1)
        def _():
            tail_y().wait()

        ytail[...] = mlp(_unfold_groups(xtail[...]))
        tail_y().start()
        ypend[2] = 1

    @pl.when(e == n_exp - 1)
    def _():
        for slot in range(2):
            @pl.when(ypend[slot] == 1)
            def _():
                y_copy(0, slot).wait()

        @pl.when(ypend[2] == 1)
        def _():
            tail_y().wait()

        ytail[...] = jnp.zeros(ytail.shape, _BF16)

        def start(q, carry):
            pltpu.make_async_copy(ytail, ys_ref.at[pl.ds(q * hg, hg)], tsem.at[1]).start()
            return carry

        def drain(q, carry):
            pltpu.make_async_copy(ytail, ys_ref.at[pl.ds(q * hg, hg)], tsem.at[1]).wait()
            return carry

        lax.fori_loop(nv_ref[0], n_half, start, 0)
        lax.fori_loop(nv_ref[0], n_half, drain, 0)


def _experts_call(nb, tail, r0, n_valid, xs, wgu, bgu, wdn, bdn, bm, n_half):
    n_exp, d, d_gu = wgu.shape
    d_exp = wdn.shape[1]
    group_tile = (2 * SUBLANES, d // 2)
    return pl.pallas_call(
        functools.partial(_experts_kernel, n_half=n_half),
        grid_spec=pltpu.PrefetchScalarGridSpec(
            num_scalar_prefetch=4,
            grid=(n_exp,),
            in_specs=[pl.BlockSpec(memory_space=pl.ANY),
                      pl.BlockSpec((1, d, d_gu), lambda e, *_: (e, 0, 0)),
                      pl.BlockSpec((1, 1, d_gu), lambda e, *_: (e, 0, 0)),
                      pl.BlockSpec((1, d_exp, d), lambda e, *_: (e, 0, 0)),
                      pl.BlockSpec((1, 1, d), lambda e, *_: (e, 0, 0))],
            out_specs=pl.BlockSpec(memory_space=pl.ANY),
            scratch_shapes=[pltpu.VMEM((d, d_gu), _BF16), pltpu.VMEM((d_exp, d), _BF16),
                            pltpu.VMEM((2, bm // SUBLANES) + group_tile, _BF16),
                            pltpu.VMEM((2, bm // SUBLANES) + group_tile, _BF16),
                            pltpu.VMEM((bm // 2 // SUBLANES,) + group_tile, _BF16),
                            pltpu.VMEM((bm // 2 // SUBLANES,) + group_tile, _BF16),
                            pltpu.SemaphoreType.DMA((2,)), pltpu.SemaphoreType.DMA((2,)),
                            pltpu.SemaphoreType.DMA((2,)), pltpu.SMEM((3,), jnp.int32)]),
        out_shape=jax.ShapeDtypeStruct((n_half * (bm // 2) // SUBLANES,) + group_tile, _BF16),
        compiler_params=pltpu.CompilerParams(
            dimension_semantics=("arbitrary",), vmem_limit_bytes=VMEM_LIMIT),
        name="experts",
    )(nb, tail, r0, n_valid, xs, wgu, bgu, wdn, bdn)


def _combine_kernel(src_ref, nxt_ref, slot_ref, gate_ref, h1_ref, gfin_ref, ys_ref, out_ref,
                    ybuf, sem):
    i = pl.program_id(0)
    n_sub, nq = ybuf.shape[:2]
    ns = nq * SUBLANES
    ts = h1_ref.shape[0] // n_sub

    def gather(tab_ref, h_from, h_to):
        def issue(q, carry):
            pltpu.make_async_copy(ys_ref.at[tab_ref[0, h_from, q]], ybuf.at[h_to, q],
                                  sem.at[h_to]).start()
            return carry

        for q in range(nq):
            issue(q, 0)

    @pl.when(i == 0)
    def _():
        gather(src_ref, 0, 0)

    for h in range(n_sub):
        if h + 1 < n_sub:
            gather(src_ref, h + 1, h + 1)
        pltpu.make_async_copy(ys_ref.at[pl.ds(0, nq)], ybuf.at[h], sem.at[h]).wait()

        slot = slot_ref[h * ts:(h + 1) * ts, :]
        gate = gate_ref[h * ts:(h + 1) * ts, :]
        cols = lax.broadcasted_iota(jnp.int32, (ts, ns), 1)
        g = jnp.zeros((ts, ns), _F32)
        for k in range(TOP_K):
            g = jnp.where(cols == slot[:, k:k + 1], gate[:, k:k + 1], g)
        yb = _unfold_groups(ybuf[h])
        if h + 1 == n_sub:
            @pl.when(i < pl.num_programs(0) - 1)
            def _():
                gather(nxt_ref, 0, 0)
        acc = h1_ref[h * ts:(h + 1) * ts, :] + _dot(g.astype(_BF16), yb)
        out_ref[h * ts:(h + 1) * ts, :] = _rms(acc, gfin_ref[...])


def _combine_call(src, slot_t, gate_t, h1, gfin, ys, tq, ns):
    n_tok, d = h1.shape
    n_steps, n_sub, nq = src.shape
    table = lambda f: pl.BlockSpec((1, n_sub, nq), f, memory_space=pltpu.SMEM)
    return pl.pallas_call(
        _combine_kernel,
        grid=(n_steps,),
        in_specs=[table(lambda i: (i, 0, 0)),
                  table(lambda i: (jnp.minimum(i + 1, n_steps - 1), 0, 0)),
                  pl.BlockSpec((tq, TOP_K), lambda i: (i, 0)),
                  pl.BlockSpec((tq, TOP_K), lambda i: (i, 0)),
                  pl.BlockSpec((tq, d), lambda i: (i, 0)),
                  pl.BlockSpec((1, d), lambda i: (0, 0)),
                  pl.BlockSpec(memory_space=pl.ANY)],
        out_specs=pl.BlockSpec((tq, d), lambda i: (i, 0)),
        out_shape=jax.ShapeDtypeStruct((n_tok, d), _F32),
        scratch_shapes=[pltpu.VMEM((n_sub, ns // SUBLANES, 2 * SUBLANES, d // 2), _BF16),
                        pltpu.SemaphoreType.DMA((n_sub,))],
        compiler_params=pltpu.CompilerParams(
            dimension_semantics=("arbitrary",), vmem_limit_bytes=VMEM_LIMIT),
        name="combine",
    )(src, src, slot_t, gate_t, h1, gfin, ys)


def kernel(x, meta_tokens, norm_mix_g, w_in, b_in, w_pool_grp, pool_scale, w_dwconv, b_dwconv, conv_ln_g, conv_ln_b, w_pw2, b_pw2, w_out, norm_ffn_g, router_w, router_b, w_gate_up, b_gate_up, w_down, b_down, norm_final_g):
    bsz, seq, d = x.shape
    assert w_in.shape[0] == 1, "one layer"
    tq = min(MIX_TILE, seq)
    ts = min(SORT_TILE, tq)
    rt = min(ROW_STEP, seq)
    n_sub = rt // ts
    bm = EXPERT_BLOCK
    n_tok = bsz * seq
    n_sort = n_tok // ts
    ns = -(-(ts * TOP_K + N_EXPERTS * (SUBLANES - 1)) // LANES) * LANES
    nq = ns // SUBLANES
    hb = bm // 2
    n_main = (n_tok * TOP_K + n_sort * N_EXPERTS * (SUBLANES - 1)) // hb + N_EXPERTS
    n_spill = -(-(n_sub * ns) // hb)
    i32 = jnp.int32

    row = lambda a: a.reshape(1, -1)
    wr_t = router_w[0].T
    wr_hi = wr_t.astype(_BF16)
    wr_lo = (wr_t - wr_hi.astype(_F32)).astype(_BF16)
    tri = (jnp.arange(ts)[:, None] < jnp.arange(ts)[None, :]).astype(_BF16)
    lmat = (jnp.arange(N_EXPERTS)[None, :] < jnp.arange(N_EXPERTS)[:, None]).astype(_BF16)

    h1, hn, gate, slot, segend, off, tot = _mixer_call(
        x, meta_tokens, row(norm_mix_g[0]), w_in[0].astype(_BF16), row(b_in[0]),
        w_pool_grp[0].astype(_BF16), row(pool_scale[0]), w_dwconv[0], row(b_dwconv[0]),
        row(conv_ln_g[0]), row(conv_ln_b[0]), w_pw2[0].astype(_BF16), row(b_pw2[0]),
        w_out[0].astype(_BF16), row(norm_ffn_g[0]), jnp.concatenate([wr_hi, wr_lo], axis=0),
        router_b[0].reshape(N_EXPERTS, 1), tri, lmat, tq)

    segend = segend[:, :, 0].astype(i32)
    off = off[:, :, 0].astype(i32)
    total = tot[:, 0].astype(i32)
    padded = (total + hb - 1) // hb * hb
    pend = jnp.cumsum(padded)
    pstart = pend - padded
    n_valid = pend[-1] // hb
    fill = jnp.stack([(pstart + total) // SUBLANES, pend // SUBLANES,
                      jnp.full_like(pend, n_valid)]).astype(i32)

    q_row = jnp.arange(nq, dtype=i32) * SUBLANES
    e_q = jnp.sum((segend[:, None, :] <= q_row[None, :, None]).astype(i32), axis=-1)
    used = q_row[None, :] < segend[:, -1:]
    hot = jnp.minimum(e_q, N_EXPERTS - 1)[..., None] == jnp.arange(N_EXPERTS, dtype=i32)
    to_row = jnp.sum(jnp.where(hot, (pstart[None, :] + off)[:, None, :], 0), axis=-1) + q_row
    spill = n_main * hb + (jnp.arange(n_sort, dtype=i32) % n_sub)[:, None] * ns + q_row
    dst = (jnp.where(used, to_row, spill) // SUBLANES).reshape(n_sort // n_sub, n_sub, nq)
    src = (jnp.where(used, to_row, 0) // SUBLANES).reshape(n_sort // n_sub, n_sub, nq)

    slot_t = slot.transpose(0, 2, 1).reshape(n_tok, TOP_K)
    gate_t = gate.transpose(0, 2, 1).reshape(n_tok, TOP_K)

    xs = _dispatch_call(fill, dst, slot, hn, (n_main + n_spill) * hb // SUBLANES, rt, ns, hb)
    ys = _experts_call((padded // bm).astype(i32), (padded // hb % 2).astype(i32),
                       (pstart // SUBLANES).astype(i32), n_valid.reshape(1).astype(i32), xs,
                       w_gate_up[0], b_gate_up[0].reshape(N_EXPERTS, 1, -1),
                       w_down[0], b_down[0].reshape(N_EXPERTS, 1, -1), bm, n_main)
    out = _combine_call(src, slot_t, gate_t, h1, row(norm_final_g), ys, rt, ns)
    return out.reshape(bsz, seq, d)
```

```python
import functools

import jax
import jax.numpy as jnp
from jax import lax
from jax.experimental import pallas as pl
from jax.experimental.pallas import tpu as pltpu

N_META = 16
POOL_WINDOWS = (2, 4, 8, 16)
POOL_GROUP_DIM = 128
POOL_OUT_DIM = 256
POOL_WIDTH = POOL_GROUP_DIM * len(POOL_WINDOWS)
CONV_WIDTH = 512
CONV_KERNEL = 31
N_EXPERTS = 32
TOP_K = 4
SWIGLU_LIMIT = 7.0
SWIGLU_ALPHA = 1.702
RMS_EPS = 1e-5
LN_EPS = 1e-5

LANES = 128
SUBLANES = 8

MIX_TILE = 512
POOL_HALO = 16
CONV_HALO = 32
CONV_ROWS = 64
SORT_TILE = 256
ROW_STEP = 1024
EXPERT_BLOCK = 512
VMEM_LIMIT = 56 * 1024 * 1024

_F32 = jnp.float32
_BF16 = jnp.bfloat16


def _rms(x, g):
    return x * lax.rsqrt(jnp.mean(x * x, axis=-1, keepdims=True) + RMS_EPS) * g


def _sigmoid(x):
    return 1.0 / (1.0 + jnp.exp(-x))


def _dot(a, b):
    return jnp.dot(a, b, preferred_element_type=_F32)


def _fold_groups(v):
    n, d = v.shape
    left = v[:, :d // 2].reshape(n // SUBLANES, SUBLANES, d // 2)
    right = v[:, d // 2:].reshape(n // SUBLANES, SUBLANES, d // 2)
    return jnp.concatenate([left, right], axis=1).astype(_BF16)


def _unfold_groups(w):
    g, _, half = w.shape
    f = w.astype(_F32)
    left = f[:, :SUBLANES, :].reshape(g * SUBLANES, half)
    right = f[:, SUBLANES:, :].reshape(g * SUBLANES, half)
    return jnp.concatenate([left, right], axis=-1).astype(_BF16)


def _mixer_kernel(x_ref, meta_ref, gmix_ref, win_ref, bin_ref, wpool_ref, pscale_ref,
                  kdw_ref, bdw_ref, lng_ref, lnb_ref, wpw2_ref, bpw2_ref, wout_ref,
                  gffn_ref, wr_ref, rb_ref, tri_ref, lmat_ref,
                  h1_ref, hn_ref, gate_ref, slot_ref, segend_ref, off_ref, tot_ref,
                  ubuf, abuf, base, hn32):
    b = pl.program_id(0)
    j = pl.program_id(1)
    tq = x_ref.shape[1]
    c0 = POOL_WIDTH
    c1 = c0 + CONV_WIDTH
    c2 = c1 + CONV_WIDTH
    d_model = x_ref.shape[2]
    c3 = c2 + d_model
    n_slab = CONV_WIDTH // LANES

    @pl.when((b == 0) & (j == 0))
    def _():
        base[...] = jnp.zeros_like(base)

    @pl.when(j == 0)
    def _():
        xm = _rms(meta_ref[...], gmix_ref[...]).astype(_BF16)
        pm = _dot(xm, win_ref[:, 0:c2]) + bin_ref[:, 0:c2]
        um = pm[:, 0:c0]
        am = pm[:, c0:c1] * _sigmoid(pm[:, c1:c2])
        for s in range(n_slab):
            ubuf[s, 0:POOL_HALO, :] = um[:, s * LANES:(s + 1) * LANES]
            abuf[s, 0:CONV_HALO - N_META, :] = jnp.zeros((CONV_HALO - N_META, LANES), _F32)
            abuf[s, CONV_HALO - N_META:CONV_HALO, :] = am[:, s * LANES:(s + 1) * LANES]

    ts = tri_ref.shape[0]
    n_sub = tq // ts
    nt_dims = (((1,), (1,)), ((), ()))
    eidx = lax.broadcasted_iota(jnp.int32, (N_EXPERTS, ts), 0).astype(_F32)

    def proj_stage(h):
        r0 = h * ts
        xn = _rms(x_ref[0, r0:r0 + ts, :], gmix_ref[...]).astype(_BF16)
        u = _dot(xn, win_ref[:, 0:c0]) + bin_ref[:, 0:c0]
        v = _dot(xn, win_ref[:, c0:c1]) + bin_ref[:, c0:c1]
        gt = _dot(xn, win_ref[:, c1:c2]) + bin_ref[:, c1:c2]
        a = v * _sigmoid(gt)
        for s in range(n_slab):
            ubuf[s, POOL_HALO + r0:POOL_HALO + r0 + ts, :] = u[:, s * LANES:(s + 1) * LANES]
            abuf[s, CONV_HALO + r0:CONV_HALO + r0 + ts, :] = a[:, s * LANES:(s + 1) * LANES]
        return xn

    def conv_stage(h):
        r0 = h * ts
        conv = []
        for s in range(n_slab):
            lo = s * LANES
            rows = []
            for rc in range(r0, r0 + ts, CONV_ROWS):
                acc = jnp.broadcast_to(bdw_ref[:, lo:lo + LANES], (CONV_ROWS, LANES))
                for tap in range(CONV_KERNEL):
                    start = CONV_HALO - (CONV_KERNEL - 1) + tap + rc
                    acc = acc + kdw_ref[tap:tap + 1, lo:lo + LANES] * abuf[s, start:start + CONV_ROWS, :]
                rows.append(acc)
            conv.append(jnp.concatenate(rows, axis=0))
        return jnp.concatenate(conv, axis=-1)

    def mix_stage(h, c, xn):
        r0 = h * ts
        g_a = _dot(xn, win_ref[:, c2:c3]) + bin_ref[:, c2:c3]
        g_b = _dot(xn, win_ref[:, c3:]) + bin_ref[:, c3:]
        ya = []
        for g, w in enumerate(POOL_WINDOWS):
            ug = ubuf[g, POOL_HALO + r0:POOL_HALO + r0 + ts, :]
            acc = ug
            for back in range(1, w):
                acc = acc + ubuf[g, POOL_HALO + r0 - back:POOL_HALO + r0 - back + ts, :]
            dg = acc * (1.0 / w) - ug
            ya.append(_dot(dg.astype(_BF16), wpool_ref[g]))
        y_a = jnp.concatenate(ya, axis=-1) * pscale_ref[...]

        mu = jnp.mean(c, axis=-1, keepdims=True)
        cc = c - mu
        var = jnp.mean(cc * cc, axis=-1, keepdims=True)
        cn = cc * lax.rsqrt(var + LN_EPS) * lng_ref[...] + lnb_ref[...]
        sw = cn * _sigmoid(cn)
        y_b = _dot(sw.astype(_BF16), wpw2_ref[...]) + bpw2_ref[...]

        mixed = _sigmoid(g_a) * y_a + _sigmoid(g_b) * y_b
        h1 = x_ref[0, r0:r0 + ts, :] + _dot(mixed.astype(_BF16), wout_ref[...])
        h1_ref[r0:r0 + ts, :] = h1
        hn = _rms(h1, gffn_ref[...])
        hn32[r0:r0 + ts, :] = hn
        hn_ref[r0:r0 + ts, :] = hn.astype(_BF16)

    def route_stage(h):
        r0 = h * ts
        hn = hn32[r0:r0 + ts, :]
        hn_hi = hn.astype(_BF16)
        hn_lo = (hn - hn_hi.astype(_F32)).astype(_BF16)
        l_hi = lax.dot_general(wr_ref[...], hn_hi, nt_dims, preferred_element_type=_F32)
        l_lo = lax.dot_general(wr_ref[0:N_EXPERTS, :], hn_lo, nt_dims, preferred_element_type=_F32)
        work = l_hi[0:N_EXPERTS] + l_hi[N_EXPERTS:2 * N_EXPERTS] + l_lo + rb_ref[...]

        vals, hots = [], []
        for _ in range(TOP_K):
            m = jnp.max(work, axis=0, keepdims=True)
            first = jnp.min(jnp.where(work == m, eidx, float(N_EXPERTS)), axis=0, keepdims=True)
            hot = eidx == first
            vals.append(m)
            hots.append(hot)
            work = jnp.where(hot, -jnp.inf, work)
        exps = [jnp.exp(vk - vals[0]) for vk in vals]
        denom = exps[0] + exps[1] + exps[2] + exps[3]
        gate_ref[0, :, r0:r0 + ts] = jnp.concatenate([ek / denom for ek in exps], axis=0)

        chosen = jnp.where(hots[0] | hots[1] | hots[2] | hots[3], 1.0, 0.0)
        groups = jnp.floor((jnp.sum(chosen, axis=1, keepdims=True) + (SUBLANES - 1)) * (1.0 / SUBLANES))
        groups = jnp.broadcast_to(groups, (N_EXPERTS, LANES))
        seg = _dot(lmat_ref[...], groups.astype(_BF16)) * float(SUBLANES)
        place = seg[:, 0:1] + _dot(chosen.astype(_BF16), tri_ref[...])
        slots = [jnp.sum(jnp.where(m, place, 0.0), axis=0, keepdims=True) for m in hots]
        slot_ref[h] = jnp.concatenate(slots, axis=0).astype(jnp.int32)
        segend_ref[h] = seg + groups * float(SUBLANES)
        off_ref[h] = base[...] - seg
        base[...] = base[...] + groups * float(SUBLANES)

    xns = [proj_stage(h) for h in range(n_sub)]
    for h in range(n_sub):
        mix_stage(h, conv_stage(h), xns[h])
        if h > 0:
            route_stage(h - 1)
    route_stage(n_sub - 1)
    tot_ref[...] = base[...]

    for s in range(n_slab):
        ubuf[s, 0:POOL_HALO, :] = ubuf[s, tq:tq + POOL_HALO, :]
        abuf[s, 0:CONV_HALO, :] = abuf[s, tq:tq + CONV_HALO, :]


def _mixer_call(x, meta, gmix, win, bin_, wpool, pscale, kdw, bdw, lng, lnb, wpw2, bpw2,
                wout, gffn, wr, rb, tri, lmat, tq):
    bsz, seq, d = x.shape
    nj = seq // tq
    n_tok = bsz * seq
    n_tiles = bsz * nj
    ts = tri.shape[0]
    n_sub = tq // ts

    def full(arr):
        nd = arr.ndim
        return pl.BlockSpec(arr.shape, lambda b, j, _n=nd: (0,) * _n)

    def tile3(k, w):
        return pl.BlockSpec((k, TOP_K, w), lambda b, j: (b * nj + j, 0, 0))

    per_expert = pl.BlockSpec((n_sub, N_EXPERTS, LANES), lambda b, j: (b * nj + j, 0, 0))
    rows = pl.BlockSpec((tq, d), lambda b, j: (b * nj + j, 0))
    consts = (meta, gmix, win, bin_, wpool, pscale, kdw, bdw, lng, lnb, wpw2, bpw2, wout,
              gffn, wr, rb, tri, lmat)
    return pl.pallas_call(
        _mixer_kernel,
        grid=(bsz, nj),
        in_specs=[pl.BlockSpec((1, tq, d), lambda b, j: (b, j, 0))] + [full(c) for c in consts],
        out_specs=[rows, rows, tile3(1, tq), tile3(n_sub, ts), per_expert, per_expert,
                   pl.BlockSpec((N_EXPERTS, LANES), lambda b, j: (0, 0))],
        out_shape=[jax.ShapeDtypeStruct((n_tok, d), _F32),
                   jax.ShapeDtypeStruct((n_tok, d), _BF16),
                   jax.ShapeDtypeStruct((n_tiles, TOP_K, tq), _F32),
                   jax.ShapeDtypeStruct((n_tiles * n_sub, TOP_K, ts), jnp.int32),
                   jax.ShapeDtypeStruct((n_tiles * n_sub, N_EXPERTS, LANES), _F32),
                   jax.ShapeDtypeStruct((n_tiles * n_sub, N_EXPERTS, LANES), _F32),
                   jax.ShapeDtypeStruct((N_EXPERTS, LANES), _F32)],
        scratch_shapes=[pltpu.VMEM((CONV_WIDTH // LANES, POOL_HALO + tq, LANES), _F32),
                        pltpu.VMEM((CONV_WIDTH // LANES, CONV_HALO + tq, LANES), _F32),
                        pltpu.VMEM((N_EXPERTS, LANES), _F32),
                        pltpu.VMEM((tq, d), _F32)],
        compiler_params=pltpu.CompilerParams(
            dimension_semantics=("arbitrary", "arbitrary"), vmem_limit_bytes=VMEM_LIMIT),
        name="mixer",
    )(x, *consts)


def _dispatch_kernel(fill_ref, dst_ref, slot_ref, hn_ref, xs_ref, sbuf, zblk, sem, zsem, *, bm):
    i = pl.program_id(0)
    n_sub, _, ts = slot_ref.shape
    nq = sbuf.shape[1]
    ns = nq * SUBLANES
    bg = bm // SUBLANES
    n_blocks = xs_ref.shape[0] // bg

    def zero_fill(lo, hi, size, wait):
        def step(q, carry):
            cp = pltpu.make_async_copy(zblk.at[pl.ds(0, size)], xs_ref.at[pl.ds(q * size, size)], zsem)
            cp.wait() if wait else cp.start()
            return carry

        lax.fori_loop(lo, hi, step, 0)

    @pl.when(i == 0)
    def _():
        zblk[...] = jnp.zeros_like(zblk)
        for wait in (False, True):
            for e in range(N_EXPERTS):
                zero_fill(fill_ref[0, e], fill_ref[1, e], 1, wait)
            zero_fill(fill_ref[2, 0], n_blocks, bg, wait)

    for h in range(n_sub):
        @pl.when(i > 0)
        def _():
            pltpu.make_async_copy(sbuf.at[h], xs_ref.at[pl.ds(0, nq)], sem.at[h]).wait()

        slot = slot_ref[h]
        rows = lax.broadcasted_iota(jnp.int32, (ns, ts), 0)
        sel = jnp.zeros((ns, ts), _F32)
        for k in range(TOP_K):
            sel = jnp.where(rows == slot[k:k + 1, :], 1.0, sel)
        sbuf[h] = _fold_groups(_dot(sel.astype(_BF16), hn_ref[h * ts:(h + 1) * ts, :]))

        def issue(q, carry):
            pltpu.make_async_copy(sbuf.at[h, q], xs_ref.at[dst_ref[0, h, q]], sem.at[h]).start(
                priority=q % 2)
            return carry

        for q in range(nq):
            issue(q, 0)

    @pl.when(i == pl.num_programs(0) - 1)
    def _():
        for h in range(n_sub):
            pltpu.make_async_copy(sbuf.at[h], xs_ref.at[pl.ds(0, nq)], sem.at[h]).wait()


def _dispatch_call(fill, dst, slot, hn, n_groups, tq, ns, bm):
    n_tok, d = hn.shape
    n_sub, ts = dst.shape[1], slot.shape[2]
    return pl.pallas_call(
        functools.partial(_dispatch_kernel, bm=bm),
        grid_spec=pltpu.PrefetchScalarGridSpec(
            num_scalar_prefetch=1,
            grid=(n_tok // tq,),
            in_specs=[pl.BlockSpec((1, n_sub, ns // SUBLANES), lambda i, f: (i, 0, 0),
                                   memory_space=pltpu.SMEM),
                      pl.BlockSpec((n_sub, TOP_K, ts), lambda i, f: (i, 0, 0)),
                      pl.BlockSpec((tq, d), lambda i, f: (i, 0))],
            out_specs=pl.BlockSpec(memory_space=pl.ANY),
            scratch_shapes=[pltpu.VMEM((n_sub, ns // SUBLANES, 2 * SUBLANES, d // 2), _BF16),
                            pltpu.VMEM((bm // SUBLANES, 2 * SUBLANES, d // 2), _BF16),
                            pltpu.SemaphoreType.DMA((n_sub,)), pltpu.SemaphoreType.DMA]),
        out_shape=jax.ShapeDtypeStruct((n_groups, 2 * SUBLANES, d // 2), _BF16),
        compiler_params=pltpu.CompilerParams(
            dimension_semantics=("arbitrary",), vmem_limit_bytes=VMEM_LIMIT),
        name="dispatch",
    )(fill, dst, slot, hn)


def _experts_kernel(nb_ref, tl_ref, r0_ref, nv_ref, xs_ref, wgu_ref, bgu_ref, wdn_ref, bdn_ref,
                    ys_ref, wgu_bf, wdn_bf, xbuf, ybuf, xtail, ytail, xsem, ysem, tsem, ypend,
                    *, n_half):
    e = pl.program_id(0)
    n_exp = pl.num_programs(0)
    bg = xbuf.shape[1]
    hg = xtail.shape[0]
    d_exp = wdn_ref.shape[1]
    nb = nb_ref[e]
    has_tail = tl_ref[e] == 1

    @pl.when(e == 0)
    def _():
        ypend[0] = 0
        ypend[1] = 0
        ypend[2] = 0

    def x_copy(ex, j, slot):
        at = r0_ref[ex] + j * bg
        return pltpu.make_async_copy(xs_ref.at[pl.ds(at, bg)], xbuf.at[slot], xsem.at[slot])

    def y_copy(j, slot):
        at = r0_ref[e] + j * bg
        return pltpu.make_async_copy(ybuf.at[slot], ys_ref.at[pl.ds(at, bg)], ysem.at[slot])

    def tail_x():
        return pltpu.make_async_copy(xs_ref.at[pl.ds(r0_ref[e] + nb * bg, hg)], xtail, tsem.at[0])

    def tail_y():
        return pltpu.make_async_copy(ytail, ys_ref.at[pl.ds(r0_ref[e] + nb * bg, hg)], tsem.at[1])

    def mlp(xb):
        gu = _dot(xb, wgu_bf[...]) + bgu_ref[0]
        gate = jnp.minimum(gu[:, 0:d_exp], SWIGLU_LIMIT)
        up = jnp.clip(gu[:, d_exp:], -SWIGLU_LIMIT, SWIGLU_LIMIT)
        act = (up + 1.0) * (gate * _sigmoid(gate * SWIGLU_ALPHA))
        return _fold_groups(_dot(act.astype(_BF16), wdn_bf[...]) + bdn_ref[0])

    @pl.when((e == 0) & (nb > 0))
    def _():
        x_copy(e, 0, 0).start()

    @pl.when(has_tail)
    def _():
        tail_x().start()

    @pl.when((nb > 0) | has_tail)
    def _():
        wgu_bf[...] = wgu_ref[0].astype(_BF16)
        wdn_bf[...] = wdn_ref[0].astype(_BF16)

    def block(j, carry):
        slot = lax.rem(j, 2)
        x_copy(e, j, slot).wait()

        @pl.when(j + 1 < nb)
        def _():
            x_copy(e, j + 1, 1 - slot).start()

        @pl.when(ypend[slot] == 1)
        def _():
            y_copy(j, slot).wait()

        ybuf[slot] = mlp(_unfold_groups(xbuf[slot]))
        y_copy(j, slot).start()
        ypend[slot] = 1
        return carry

    lax.fori_loop(0, nb, block, 0)

    nxt = jnp.minimum(e + 1, n_exp - 1)

    @pl.when((e + 1 < n_exp) & (nb_ref[nxt] > 0))
    def _():
        x_copy(nxt, 0, 0).start()

    @pl.when(has_tail)
    def _():
        tail_x().wait()

        @pl.when(ypend[2] == 1)
        def _():
            tail_y().wait()

        ytail[...] = mlp(_unfold_groups(xtail[...]))
        tail_y().start()
        ypend[2] = 1

    @pl.when(e == n_exp - 1)
    def _():
        for slot in range(2):
            @pl.when(ypend[slot] == 1)
            def _():
                y_copy(0, slot).wait()

        @pl.when(ypend[2] == 1)
        def _():
            tail_y().wait()

        ytail[...] = jnp.zeros(ytail.shape, _BF16)

        def start(q, carry):
            pltpu.make_async_copy(ytail, ys_ref.at[pl.ds(q * hg, hg)], tsem.at[1]).start()
            return carry

        def drain(q, carry):
            pltpu.make_async_copy(ytail, ys_ref.at[pl.ds(q * hg, hg)], tsem.at[1]).wait()
            return carry

        lax.fori_loop(nv_ref[0], n_half, start, 0)
        lax.fori_loop(nv_ref[0], n_half, drain, 0)


def _experts_call(nb, tail, r0, n_valid, xs, wgu, bgu, wdn, bdn, bm, n_half):
    n_exp, d, d_gu = wgu.shape
    d_exp = wdn.shape[1]
    group_tile = (2 * SUBLANES, d // 2)
    return pl.pallas_call(
        functools.partial(_experts_kernel, n_half=n_half),
        grid_spec=pltpu.PrefetchScalarGridSpec(
            num_scalar_prefetch=4,
            grid=(n_exp,),
            in_specs=[pl.BlockSpec(memory_space=pl.ANY),
                      pl.BlockSpec((1, d, d_gu), lambda e, *_: (e, 0, 0)),
                      pl.BlockSpec((1, 1, d_gu), lambda e, *_: (e, 0, 0)),
                      pl.BlockSpec((1, d_exp, d), lambda e, *_: (e, 0, 0)),
                      pl.BlockSpec((1, 1, d), lambda e, *_: (e, 0, 0))],
            out_specs=pl.BlockSpec(memory_space=pl.ANY),
            scratch_shapes=[pltpu.VMEM((d, d_gu), _BF16), pltpu.VMEM((d_exp, d), _BF16),
                            pltpu.VMEM((2, bm // SUBLANES) + group_tile, _BF16),
                            pltpu.VMEM((2, bm // SUBLANES) + group_tile, _BF16),
                            pltpu.VMEM((bm // 2 // SUBLANES,) + group_tile, _BF16),
                            pltpu.VMEM((bm // 2 // SUBLANES,) + group_tile, _BF16),
                            pltpu.SemaphoreType.DMA((2,)), pltpu.SemaphoreType.DMA((2,)),
                            pltpu.SemaphoreType.DMA((2,)), pltpu.SMEM((3,), jnp.int32)]),
        out_shape=jax.ShapeDtypeStruct((n_half * (bm // 2) // SUBLANES,) + group_tile, _BF16),
        compiler_params=pltpu.CompilerParams(
            dimension_semantics=("arbitrary",), vmem_limit_bytes=VMEM_LIMIT),
        name="experts",
    )(nb, tail, r0, n_valid, xs, wgu, bgu, wdn, bdn)


def _combine_kernel(src_ref, nxt_ref, slot_ref, gate_ref, h1_ref, gfin_ref, ys_ref, out_ref,
                    ybuf, sem):
    i = pl.program_id(0)
    n_sub, nq = ybuf.shape[:2]
    ns = nq * SUBLANES
    ts = h1_ref.shape[0] // n_sub

    def gather(tab_ref, h_from, h_to):
        def issue(q, carry):
            pltpu.make_async_copy(ys_ref.at[tab_ref[0, h_from, q]], ybuf.at[h_to, q],
                                  sem.at[h_to]).start(priority=q % 2)
            return carry

        for q in range(nq):
            issue(q, 0)

    @pl.when(i == 0)
    def _():
        gather(src_ref, 0, 0)

    for h in range(n_sub):
        if h + 1 < n_sub:
            gather(src_ref, h + 1, h + 1)
        pltpu.make_async_copy(ys_ref.at[pl.ds(0, nq)], ybuf.at[h], sem.at[h]).wait()

        slot = slot_ref[h * ts:(h + 1) * ts, :]
        gate = gate_ref[h * ts:(h + 1) * ts, :]
        cols = lax.broadcasted_iota(jnp.int32, (ts, ns), 1)
        g = jnp.zeros((ts, ns), _F32)
        for k in range(TOP_K):
            g = jnp.where(cols == slot[:, k:k + 1], gate[:, k:k + 1], g)
        yb = _unfold_groups(ybuf[h])
        if h + 1 == n_sub:
            @pl.when(i < pl.num_programs(0) - 1)
            def _():
                gather(nxt_ref, 0, 0)
        acc = h1_ref[h * ts:(h + 1) * ts, :] + _dot(g.astype(_BF16), yb)
        out_ref[h * ts:(h + 1) * ts, :] = _rms(acc, gfin_ref[...])


def _combine_call(src, slot_t, gate_t, h1, gfin, ys, tq, ns):
    n_tok, d = h1.shape
    n_steps, n_sub, nq = src.shape
    table = lambda f: pl.BlockSpec((1, n_sub, nq), f, memory_space=pltpu.SMEM)
    return pl.pallas_call(
        _combine_kernel,
        grid=(n_steps,),
        in_specs=[table(lambda i: (i, 0, 0)),
                  table(lambda i: (jnp.minimum(i + 1, n_steps - 1), 0, 0)),
                  pl.BlockSpec((tq, TOP_K), lambda i: (i, 0)),
                  pl.BlockSpec((tq, TOP_K), lambda i: (i, 0)),
                  pl.BlockSpec((tq, d), lambda i: (i, 0)),
                  pl.BlockSpec((1, d), lambda i: (0, 0)),
                  pl.BlockSpec(memory_space=pl.ANY)],
        out_specs=pl.BlockSpec((tq, d), lambda i: (i, 0)),
        out_shape=jax.ShapeDtypeStruct((n_tok, d), _F32),
        scratch_shapes=[pltpu.VMEM((n_sub, ns // SUBLANES, 2 * SUBLANES, d // 2), _BF16),
                        pltpu.SemaphoreType.DMA((n_sub,))],
        compiler_params=pltpu.CompilerParams(
            dimension_semantics=("arbitrary",), vmem_limit_bytes=VMEM_LIMIT),
        name="combine",
    )(src, src, slot_t, gate_t, h1, gfin, ys)


def kernel(x, meta_tokens, norm_mix_g, w_in, b_in, w_pool_grp, pool_scale, w_dwconv, b_dwconv, conv_ln_g, conv_ln_b, w_pw2, b_pw2, w_out, norm_ffn_g, router_w, router_b, w_gate_up, b_gate_up, w_down, b_down, norm_final_g):
    bsz, seq, d = x.shape
    assert w_in.shape[0] == 1, "one layer"
    tq = min(MIX_TILE, seq)
    ts = min(SORT_TILE, tq)
    rt = min(ROW_STEP, seq)
    n_sub = rt // ts
    bm = EXPERT_BLOCK
    n_tok = bsz * seq
    n_sort = n_tok // ts
    ns = -(-(ts * TOP_K + N_EXPERTS * (SUBLANES - 1)) // LANES) * LANES
    nq = ns // SUBLANES
    hb = bm // 2
    n_main = (n_tok * TOP_K + n_sort * N_EXPERTS * (SUBLANES - 1)) // hb + N_EXPERTS
    n_spill = -(-(n_sub * ns) // hb)
    i32 = jnp.int32

    row = lambda a: a.reshape(1, -1)
    wr_t = router_w[0].T
    wr_hi = wr_t.astype(_BF16)
    wr_lo = (wr_t - wr_hi.astype(_F32)).astype(_BF16)
    tri = (jnp.arange(ts)[:, None] < jnp.arange(ts)[None, :]).astype(_BF16)
    lmat = (jnp.arange(N_EXPERTS)[None, :] < jnp.arange(N_EXPERTS)[:, None]).astype(_BF16)

    h1, hn, gate, slot, segend, off, tot = _mixer_call(
        x, meta_tokens, row(norm_mix_g[0]), w_in[0].astype(_BF16), row(b_in[0]),
        w_pool_grp[0].astype(_BF16), row(pool_scale[0]), w_dwconv[0], row(b_dwconv[0]),
        row(conv_ln_g[0]), row(conv_ln_b[0]), w_pw2[0].astype(_BF16), row(b_pw2[0]),
        w_out[0].astype(_BF16), row(norm_ffn_g[0]), jnp.concatenate([wr_hi, wr_lo], axis=0),
        router_b[0].reshape(N_EXPERTS, 1), tri, lmat, tq)

    segend = segend[:, :, 0].astype(i32)
    off = off[:, :, 0].astype(i32)
    total = tot[:, 0].astype(i32)
    padded = (total + hb - 1) // hb * hb
    pend = jnp.cumsum(padded)
    pstart = pend - padded
    n_valid = pend[-1] // hb
    fill = jnp.stack([(pstart + total) // SUBLANES, pend // SUBLANES,
                      jnp.full_like(pend, n_valid)]).astype(i32)

    q_row = jnp.arange(nq, dtype=i32) * SUBLANES
    e_q = jnp.sum((segend[:, None, :] <= q_row[None, :, None]).astype(i32), axis=-1)
    used = q_row[None, :] < segend[:, -1:]
    hot = jnp.minimum(e_q, N_EXPERTS - 1)[..., None] == jnp.arange(N_EXPERTS, dtype=i32)
    to_row = jnp.sum(jnp.where(hot, (pstart[None, :] + off)[:, None, :], 0), axis=-1) + q_row
    spill = n_main * hb + (jnp.arange(n_sort, dtype=i32) % n_sub)[:, None] * ns + q_row
    dst = (jnp.where(used, to_row, spill) // SUBLANES).reshape(n_sort // n_sub, n_sub, nq)
    src = (jnp.where(used, to_row, 0) // SUBLANES).reshape(n_sort // n_sub, n_sub, nq)

    slot_t = slot.transpose(0, 2, 1).reshape(n_tok, TOP_K)
    gate_t = gate.transpose(0, 2, 1).reshape(n_tok, TOP_K)

    xs = _dispatch_call(fill, dst, slot, hn, (n_main + n_spill) * hb // SUBLANES, rt, ns, hb)
    ys = _experts_call((padded // bm).astype(i32), (padded // hb % 2).astype(i32),
                       (pstart // SUBLANES).astype(i32), n_valid.reshape(1).astype(i32), xs,
                       w_gate_up[0], b_gate_up[0].reshape(N_EXPERTS, 1, -1),
                       w_down[0], b_down[0].reshape(N_EXPERTS, 1, -1), bm, n_main)
    out = _combine_call(src, slot_t, gate_t, h1, row(norm_final_g), ys, rt, ns)
    return out.reshape(bsz, seq, d)
```
